```python
import jax, jax.numpy as jnp
from jax import lax
import numpy as np

D_MODEL = 1024
BATCH = 32
SEQ = 2048
DEPTH = 4

N_A_LAYERS = DEPTH // 2
N_B_LAYERS = DEPTH - N_A_LAYERS
RMS_EPS = 1e-6
RET_HEADS = D_MODEL // 256
RET_QK_DIM = D_MODEL // RET_HEADS
RET_V_DIM = 2 * D_MODEL // RET_HEADS
RET_IN_WIDTH = 6 * D_MODEL
RET_CHUNK = 128
RET_ROT_BASE = 10000.0
GN_EPS = 1e-6
ATT_HEADS = 16
ATT_HEAD_DIM = D_MODEL // ATT_HEADS
DIL_GROUPS = ((128, 1), (512, 4), (2048, 16))
N_GROUPS = 3
ATT_BLOCK = 128
ROPE_THETA = 500000.0
ROPE_DIMS = ATT_HEAD_DIM // 4
Q_WIDTH = N_GROUPS * ATT_HEADS * ATT_HEAD_DIM
D_FF = 2816
CONV_WIDTH = 3

kernel_name = "yoco_retention_dilated_attn_convffn"


def rmsnorm(x, g):
    xf = x.astype(jnp.float32)
    y = xf * lax.rsqrt(jnp.mean(xf * xf, axis=-1, keepdims=True) + RMS_EPS)
    return (y * g).astype(x.dtype)


def apply_rotary(t, pos, inv_freq):
    n = 2 * inv_freq.shape[0]
    S = t.shape[1]
    ang = pos.astype(jnp.float32)[:, None] * inv_freq[None, :]
    bshape = (1, S) + (1,) * (t.ndim - 3) + (n // 2,)
    cos = jnp.cos(ang).reshape(bshape)
    sin = jnp.sin(ang).reshape(bshape)
    t1, t2, rest = t[..., : n // 2], t[..., n // 2: n], t[..., n:]
    rot = jnp.concatenate([t1 * cos - t2 * sin, t2 * cos + t1 * sin], axis=-1).astype(t.dtype)
    return jnp.concatenate([rot, rest], axis=-1)


def retention_chunkwise(q, k, v, log_gamma):
    B, S, H, dk = q.shape
    dv = v.shape[-1]
    C = RET_CHUNK
    nc = S // C
    qc = q.reshape(B, nc, C, H, dk)
    kc = k.reshape(B, nc, C, H, dk)
    vc = v.reshape(B, nc, C, H, dv)
    n = jnp.arange(C, dtype=jnp.float32)
    diff = n[:, None] - n[None, :]
    decay_mask = jnp.where(diff[None] >= 0,
                           jnp.exp(jnp.maximum(diff, 0.0)[None] * log_gamma[:, None, None]), 0.0)
    scores = jnp.einsum('bnqhd,bnkhd->bnhqk', qc, kc) * decay_mask
    intra = jnp.einsum('bnhqk,bnkhe->bnqhe', scores, vc)
    xi = jnp.exp((n + 1.0)[None, :] * log_gamma[:, None]).T[None, :, :, None]
    zeta = jnp.exp((C - 1.0 - n)[None, :] * log_gamma[:, None]).T[None, :, :, None]
    chunk_decay = jnp.exp(C * log_gamma)[None, :, None, None]

    def step(state, inp):
        qi, ki, vi = inp
        cross = jnp.einsum('bchd,bhde->bche', qi, state) * xi
        state = state * chunk_decay + jnp.einsum('bchd,bche->bhde', ki * zeta, vi)
        return state, cross

    state0 = jnp.zeros((B, H, dk, dv), jnp.float32)
    _, cross = lax.scan(step, state0, (jnp.moveaxis(qc, 1, 0), jnp.moveaxis(kc, 1, 0), jnp.moveaxis(vc, 1, 0)))
    out = intra + jnp.moveaxis(cross, 0, 1)
    return out.reshape(B, S, H, dv).astype(v.dtype)


def retention_mixer(h, w_in, w_out, gn_gain, pos):
    B, S, _ = h.shape
    dqk = RET_HEADS * RET_QK_DIM
    dv = RET_HEADS * RET_V_DIM
    proj = h @ w_in
    q = proj[..., :dqk].reshape(B, S, RET_HEADS, RET_QK_DIM)
    k = proj[..., dqk:2 * dqk].reshape(B, S, RET_HEADS, RET_QK_DIM) * (RET_QK_DIM ** -0.5)
    v = proj[..., 2 * dqk:2 * dqk + dv].reshape(B, S, RET_HEADS, RET_V_DIM)
    g = proj[..., 2 * dqk + dv:]
    inv_freq = 1.0 / (RET_ROT_BASE ** jnp.linspace(0.0, 1.0, RET_QK_DIM // 2, dtype=jnp.float32))
    q = apply_rotary(q, pos, inv_freq)
    k = apply_rotary(k, pos, inv_freq)
    log_gamma = jnp.log(1.0 - 2.0 ** (-5.0 - jnp.arange(RET_HEADS, dtype=jnp.float32)))
    y = retention_chunkwise(q, k, v, log_gamma).astype(jnp.float32)
    mu = jnp.mean(y, axis=-1, keepdims=True)
    var = jnp.mean(jnp.square(y - mu), axis=-1, keepdims=True)
    yn = ((y - mu) * lax.rsqrt(var + GN_EPS)).reshape(B, S, dv) * gn_gain
    return (jax.nn.silu(g) * yn.astype(g.dtype)) @ w_out


def shared_kv(x, kv_norm, w_kv, pos):
    B, S, _ = x.shape
    kv = (rmsnorm(x, kv_norm) @ w_kv).reshape(B, S, 2, N_GROUPS, ATT_HEADS, ATT_HEAD_DIM)
    inv_freq = ROPE_THETA ** (-jnp.arange(0, ROPE_DIMS, 2, dtype=jnp.float32) / ROPE_DIMS)
    k = apply_rotary(kv[:, :, 0], pos, inv_freq)
    v = kv[:, :, 1]
    return k, v


def dilated_group_attention(q, k, v, window, dilation):
    B, S, H, hd = q.shape
    blk = ATT_BLOCK
    steps = window // dilation
    span = dilation * blk
    s_pad = -(-S // span) * span
    L = s_pad // dilation
    nblk = L // blk

    def to_sub(t):
        t = jnp.pad(t, ((0, 0), (0, s_pad - S), (0, 0), (0, 0)))
        return t.reshape(B, L, dilation, H, hd).transpose(0, 2, 1, 3, 4)

    qs = to_sub(q)
    ks = jnp.pad(to_sub(k), ((0, 0), (0, 0), (blk, 0), (0, 0), (0, 0)))
    vs = jnp.pad(to_sub(v), ((0, 0), (0, 0), (blk, 0), (0, 0), (0, 0)))
    kj = jnp.arange(2 * blk)[None, :]
    dist = (jnp.arange(blk)[:, None] + blk) - kj
    band = (dist >= 0) & (dist <= steps)
    scale = hd ** -0.5

    def one_block(idx):
        r = idx // nblk
        b = idx % nblk
        qb = lax.dynamic_slice(qs, (0, r, b * blk, 0, 0), (B, 1, blk, H, hd))[:, 0]
        kb = lax.dynamic_slice(ks, (0, r, b * blk, 0, 0), (B, 1, 2 * blk, H, hd))[:, 0]
        vb = lax.dynamic_slice(vs, (0, r, b * blk, 0, 0), (B, 1, 2 * blk, H, hd))[:, 0]
        s = jnp.einsum('bqhd,bkhd->bhqk', qb, kb).astype(jnp.float32) * scale
        valid = band & (b * blk + kj - blk >= 0)
        s = jnp.where(valid, s, -jnp.inf)
        m = jnp.max(s, axis=-1, keepdims=True)
        p = jnp.exp(s - m)
        den = jnp.sum(p, axis=-1)
        o = jnp.einsum('bhqk,bkhd->bqhd', p, vb) / den.transpose(0, 2, 1)[..., None]
        lse = (m[..., 0] + jnp.log(den)).transpose(0, 2, 1)
        return o.astype(q.dtype), lse

    o, lse = lax.map(one_block, jnp.arange(dilation * nblk))
    o = o.reshape(dilation, nblk, B, blk, H, hd).transpose(2, 1, 3, 0, 4, 5).reshape(B, s_pad, H, hd)[:, :S]
    lse = lse.reshape(dilation, nblk, B, blk, H).transpose(2, 1, 3, 0, 4).reshape(B, s_pad, H)[:, :S]
    return o, lse


def dilated_mixer(h, w_q, w_o, k_sh, v_sh, pos):
    B, S, _ = h.shape
    q = (h @ w_q).reshape(B, S, N_GROUPS, ATT_HEADS, ATT_HEAD_DIM)
    inv_freq = ROPE_THETA ** (-jnp.arange(0, ROPE_DIMS, 2, dtype=jnp.float32) / ROPE_DIMS)
    q = apply_rotary(q, pos, inv_freq)
    outs, lses = [], []
    for gi, (window, dilation) in enumerate(DIL_GROUPS):
        o, l = dilated_group_attention(q[:, :, gi], k_sh[:, :, gi], v_sh[:, :, gi], window, dilation)
        outs.append(o)
        lses.append(l)
    alpha = jax.nn.softmax(jnp.stack(lses, axis=0), axis=0)
    o = jnp.einsum('gbsh,gbshd->bshd', alpha, jnp.stack(outs, axis=0).astype(jnp.float32))
    return o.astype(h.dtype).reshape(B, S, ATT_HEADS * ATT_HEAD_DIM) @ w_o


def conv_ffn(h, w_up, conv_w, conv_b, w_down):
    S = h.shape[1]
    u = h @ w_up
    gate, val = u[..., :D_FF], u[..., D_FF:]
    gp = jnp.pad(gate, ((0, 0), (CONV_WIDTH - 1, 0), (0, 0)))
    conv = conv_b
    for i in range(CONV_WIDTH):
        conv = conv + conv_w[i] * gp[:, i:i + S]
    return (jax.nn.gelu(conv, approximate=True) * val) @ w_down


def setup_inputs(seed: int = 0) -> dict:
    key = jax.random.key(seed)
    ks = jax.random.split(key, 17)
    f32 = jnp.float32

    def nrm(k, shape, fan_in):
        return jax.random.normal(k, shape, f32) * (fan_in ** -0.5)

    def gain(k, shape):
        return 1.0 + 0.02 * jax.random.normal(k, shape, f32)

    return {
        "x": jax.random.normal(ks[0], (BATCH, SEQ, D_MODEL), f32),
        "ret_w_in": nrm(ks[1], (N_A_LAYERS, D_MODEL, RET_IN_WIDTH), D_MODEL),
        "ret_w_out": nrm(ks[2], (N_A_LAYERS, RET_HEADS * RET_V_DIM, D_MODEL), RET_HEADS * RET_V_DIM),
        "ret_gn_gain": gain(ks[3], (N_A_LAYERS, RET_HEADS * RET_V_DIM)),
        "kv_norm": gain(ks[4], (D_MODEL,)),
        "att_w_kv": nrm(ks[5], (D_MODEL, 2 * Q_WIDTH), D_MODEL),
        "att_w_q": nrm(ks[6], (N_B_LAYERS, D_MODEL, Q_WIDTH), D_MODEL),
        "att_w_o": nrm(ks[7], (N_B_LAYERS, ATT_HEADS * ATT_HEAD_DIM, D_MODEL), ATT_HEADS * ATT_HEAD_DIM),
        "norm_mix_pre": gain(ks[8], (DEPTH, D_MODEL)),
        "norm_mix_post": gain(ks[9], (DEPTH, D_MODEL)),
        "norm_ffn_pre": gain(ks[10], (DEPTH, D_MODEL)),
        "norm_ffn_post": gain(ks[11], (DEPTH, D_MODEL)),
        "ffn_w_up": nrm(ks[12], (DEPTH, D_MODEL, 2 * D_FF), D_MODEL),
        "ffn_conv_w": nrm(ks[13], (DEPTH, CONV_WIDTH, D_FF), CONV_WIDTH),
        "ffn_conv_b": 0.02 * jax.random.normal(ks[14], (DEPTH, D_FF), f32),
        "ffn_w_down": nrm(ks[15], (DEPTH, D_FF, D_MODEL), D_FF),
    }


def reference(x, ret_w_in, ret_w_out, ret_gn_gain, kv_norm, att_w_kv, att_w_q, att_w_o,
              norm_mix_pre, norm_mix_post, norm_ffn_pre, norm_ffn_post,
              ffn_w_up, ffn_conv_w, ffn_conv_b, ffn_w_down):
    pos = jnp.arange(x.shape[1], dtype=jnp.int32)
    k_sh, v_sh = None, None
    for layer in range(DEPTH):
        h = rmsnorm(x, norm_mix_pre[layer])
        if layer < N_A_LAYERS:
            m = retention_mixer(h, ret_w_in[layer], ret_w_out[layer], ret_gn_gain[layer], pos)
        else:
            bi = layer - N_A_LAYERS
            m = dilated_mixer(h, att_w_q[bi], att_w_o[bi], k_sh, v_sh, pos)
        x = x + rmsnorm(m, norm_mix_post[layer])
        f = conv_ffn(rmsnorm(x, norm_ffn_pre[layer]), ffn_w_up[layer], ffn_conv_w[layer],
                     ffn_conv_b[layer], ffn_w_down[layer])
        x = x + rmsnorm(f, norm_ffn_post[layer])
        if layer == N_A_LAYERS - 1:
            k_sh, v_sh = shared_kv(x, kv_norm, att_w_kv, pos)
    return x
```

```python
import functools
import math

import jax
import jax.numpy as jnp
from jax import lax
from jax.experimental import pallas as pl
from jax.experimental.pallas import tpu as pltpu

F32 = jnp.float32
BF16 = jnp.bfloat16

LANES = 128
V7X_VMEM_BYTES = 64 * 1024 * 1024

RMS_EPS = 1e-6
GN_EPS = 1e-6
RET_HEADS = 4
RET_QK_DIM = 256
RET_V_DIM = 512
RET_ROT_BASE = 10000.0
ATT_HEADS = 16
ATT_HEAD_DIM = 64
ATT_GROUP_WIDTH = ATT_HEADS * ATT_HEAD_DIM
DIL_GROUPS = ((128, 1), (512, 4), (2048, 16))
N_GROUPS = 3
ATT_BLOCK = 128
ROPE_THETA = 500000.0
ROPE_DIMS = ATT_HEAD_DIM // 4
CONV_WIDTH = 3

TOKEN_TILE = 512
COL_CHUNK = 512
FFN_CHUNK = 256
CONV_HALO = 16
RET_CHUNK = 256
NEG_BIG = -1e30


def _vmem_limit(nbytes):
    return int(min(V7X_VMEM_BYTES - (4 << 20), max(nbytes, 16 << 20)))


def _params(n_axes, vmem_bytes):
    return pltpu.CompilerParams(dimension_semantics=("arbitrary",) * n_axes,
                                vmem_limit_bytes=_vmem_limit(vmem_bytes))


def _resident(shape):
    return pl.BlockSpec(shape, lambda *_: (0,) * len(shape), pipeline_mode=pl.Buffered(1))


def _rms(x, gain):
    return x * lax.rsqrt(jnp.mean(x * x, axis=-1, keepdims=True) + RMS_EPS) * gain


def _dot(a, b):
    return jnp.dot(a, b, preferred_element_type=F32)


def _rope_tables_kernel(rfreq_ref, afreq_ref, aneg_ref, apos_ref,
                        rcos_ref, rsin_ref, acos_ref, aneg_out_ref, apos_out_ref):
    seq = rcos_ref.shape[0]
    pos = lax.broadcasted_iota(jnp.int32, (seq, LANES), 0).astype(F32)
    rang = pos * rfreq_ref[...]
    rcos_ref[...] = jnp.cos(rang)
    rsin_ref[...] = jnp.sin(rang)
    aang = pos * afreq_ref[...]
    asin = jnp.sin(aang)
    acos_ref[...] = jnp.cos(aang)
    aneg_out_ref[...] = asin * aneg_ref[...]
    apos_out_ref[...] = asin * apos_ref[...]


def _rope_tables(seq):
    rfreq = 1.0 / (RET_ROT_BASE ** jnp.linspace(0.0, 1.0, RET_QK_DIM // 2, dtype=F32))
    inv = ROPE_THETA ** (-jnp.arange(0, ROPE_DIMS, 2, dtype=F32) / ROPE_DIMS)
    half = ROPE_DIMS // 2
    lane = jnp.arange(LANES) % ATT_HEAD_DIM
    afreq = jnp.where(lane < ROPE_DIMS, inv[lane % half], 0.0).astype(F32)
    aneg = jnp.where(lane < half, -1.0, 0.0).astype(F32)
    apos = jnp.where((lane >= half) & (lane < ROPE_DIMS), 1.0, 0.0).astype(F32)
    tab = jax.ShapeDtypeStruct((seq, LANES), F32)
    return pl.pallas_call(
        _rope_tables_kernel,
        out_shape=(tab,) * 5,
        name="rope_tables",
    )(rfreq[None, :], afreq[None, :], aneg[None, :], apos[None, :])


def _partial_rotary(t, cos, pos, neg):
    return t * cos + pltpu.roll(t, 8, 1) * pos + pltpu.roll(t, LANES - 8, 1) * neg


def _ret_inproj_kernel(x_ref, gain_ref, w_ref, cos_ref, sin_ref, q_ref, k_ref, v_ref, g_ref):
    hb = _rms(x_ref[...], gain_ref[...]).astype(BF16)
    cos = cos_ref[...]
    sin = sin_ref[...]
    dqk = RET_HEADS * RET_QK_DIM
    dv = RET_HEADS * RET_V_DIM
    half = RET_QK_DIM // 2
    n_chunks = w_ref.shape[1] // COL_CHUNK
    for c in range(n_chunks):
        col = c * COL_CHUNK
        acc = _dot(hb, w_ref[:, col:col + COL_CHUNK])
        if col < 2 * dqk:
            dst, base, scale = (q_ref, col, 1.0) if col < dqk else (k_ref, col - dqk, RET_QK_DIM ** -0.5)
            for hh in range(COL_CHUNK // RET_QK_DIM):
                a = acc[:, hh * RET_QK_DIM:hh * RET_QK_DIM + half]
                b = acc[:, hh * RET_QK_DIM + half:(hh + 1) * RET_QK_DIM]
                o = base + hh * RET_QK_DIM
                dst[:, o:o + half] = ((a * cos - b * sin) * scale).astype(BF16)
                dst[:, o + half:o + RET_QK_DIM] = ((b * cos + a * sin) * scale).astype(BF16)
        elif col < 2 * dqk + dv:
            o = col - 2 * dqk
            v_ref[:, o:o + COL_CHUNK] = acc.astype(BF16)
        else:
            o = col - 2 * dqk - dv
            g_ref[:, o:o + COL_CHUNK] = (acc / (1.0 + jnp.exp(-acc))).astype(BF16)


def _ret_inproj(x, gain, w, cos, sin, seq):
    tokens, d = x.shape
    tm = TOKEN_TILE
    tps = seq // tm
    dqk = RET_HEADS * RET_QK_DIM
    dv = RET_HEADS * RET_V_DIM
    row = lambda i: (i, 0)
    tab = pl.BlockSpec((tm, LANES), lambda i: (i % tps, 0))
    vmem = 2 * tm * d * 4 + w.size * 2 + 2 * tm * (2 * dqk + 2 * dv) * 2 + 8 * tm * COL_CHUNK * 4 + (8 << 20)
    return pl.pallas_call(
        _ret_inproj_kernel,
        grid=(tokens // tm,),
        in_specs=[pl.BlockSpec((tm, d), row), _resident((1, d)), _resident(w.shape), tab, tab],
        out_specs=[pl.BlockSpec((tm, dqk), row), pl.BlockSpec((tm, dqk), row),
                   pl.BlockSpec((tm, dv), row), pl.BlockSpec((tm, dv), row)],
        out_shape=[jax.ShapeDtypeStruct((tokens, dqk), BF16), jax.ShapeDtypeStruct((tokens, dqk), BF16),
                   jax.ShapeDtypeStruct((tokens, dv), BF16), jax.ShapeDtypeStruct((tokens, dv), BF16)],
        compiler_params=_params(1, vmem),
        name="ret_inproj",
    )(x, gain, w, cos, sin)


def _retention_kernel(lg_ref, q_ref, k_ref, v_ref, g_ref, gain_ref, o_ref, state_ref):
    head = pl.program_id(1)
    lg = lg_ref[head]
    seq = q_ref.shape[0]
    ck = RET_CHUNK
    n_chunks = seq // ck
    row = lax.broadcasted_iota(jnp.int32, (ck, ck), 0)
    col = lax.broadcasted_iota(jnp.int32, (ck, ck), 1)
    diff = (row - col).astype(F32)
    decay = jnp.where(diff >= 0.0, jnp.exp(jnp.maximum(diff, 0.0) * lg), 0.0)
    n = lax.broadcasted_iota(jnp.int32, (ck, 1), 0).astype(F32)
    xi = jnp.exp((n + 1.0) * lg)
    zeta = jnp.exp((ck - 1.0 - n) * lg)
    chunk_decay = jnp.exp(jnp.full((1, RET_V_DIM), ck, F32) * lg)
    gain = gain_ref[...]

    state_ref[...] = jnp.zeros_like(state_ref)
    for c in range(n_chunks):
        rows = pl.ds(c * ck, ck)
        qc = q_ref[rows, :]
        kc = k_ref[rows, :]
        vc = v_ref[rows, :]
        scores = lax.dot_general(qc, kc, (((1,), (1,)), ((), ())), preferred_element_type=F32)
        y = _dot((scores * decay).astype(BF16), vc)
        y = y + _dot(qc, state_ref[...].astype(BF16)) * xi
        if c + 1 < n_chunks:
            kz = (kc.astype(F32) * zeta).astype(BF16)
            upd = lax.dot_general(kz, vc, (((0,), (0,)), ((), ())), preferred_element_type=F32)
            state_ref[...] = state_ref[...] * chunk_decay + upd
        mu = jnp.mean(y, axis=-1, keepdims=True)
        yc = y - mu
        var = jnp.mean(yc * yc, axis=-1, keepdims=True)
        yn = yc * lax.rsqrt(var + GN_EPS) * gain
        o_ref[rows, :] = (g_ref[rows, :].astype(F32) * yn).astype(BF16)


def _retention(q, k, v, g, gn_gain, log_gamma, batch, seq):
    tokens = q.shape[0]
    dv = RET_HEADS * RET_V_DIM
    qk_spec = pl.BlockSpec((seq, RET_QK_DIM), lambda b, h: (b, h))
    v_spec = pl.BlockSpec((seq, RET_V_DIM), lambda b, h: (b, h))
    vmem = 2 * seq * (2 * RET_QK_DIM + 3 * RET_V_DIM) * 2 + 16 * RET_CHUNK * RET_V_DIM * 4 + (8 << 20)
    return pl.pallas_call(
        _retention_kernel,
        grid=(batch, RET_HEADS),
        in_specs=[pl.BlockSpec(memory_space=pltpu.SMEM), qk_spec, qk_spec, v_spec, v_spec,
                  pl.BlockSpec((1, RET_V_DIM), lambda b, h: (0, h))],
        out_specs=v_spec,
        out_shape=jax.ShapeDtypeStruct((tokens, dv), BF16),
        scratch_shapes=[pltpu.VMEM((RET_QK_DIM, RET_V_DIM), F32)],
        compiler_params=_params(2, vmem),
        name="retention",
    )(log_gamma, q, k, v, g, gn_gain)


def _outproj_kernel(x_ref, a_ref, w_ref, gain_ref, o_ref):
    m = _dot(a_ref[...], w_ref[...])
    o_ref[...] = x_ref[...] + _rms(m, gain_ref[...])


def _outproj(x, a, w, gain):
    tokens, d = x.shape
    kdim = a.shape[1]
    tm = TOKEN_TILE
    row = lambda i: (i, 0)
    vmem = 4 * tm * d * 4 + 2 * tm * kdim * 2 + w.size * 2 + 4 * tm * d * 4 + (8 << 20)
    return pl.pallas_call(
        _outproj_kernel,
        grid=(tokens // tm,),
        in_specs=[pl.BlockSpec((tm, d), row), pl.BlockSpec((tm, kdim), row), _resident(w.shape), _resident((1, d))],
        out_specs=pl.BlockSpec((tm, d), row),
        out_shape=jax.ShapeDtypeStruct((tokens, d), F32),
        compiler_params=_params(1, vmem),
        name="outproj",
    )(x, a, w, gain)


def _att_merge_kernel(x_ref, o0_ref, o1_ref, o2_ref, l0_ref, l1_ref, l2_ref, w_ref, gain_ref, out_ref):
    l0, l1, l2 = l0_ref[...], l1_ref[...], l2_ref[...]
    top = jnp.maximum(jnp.maximum(l0, l1), l2)
    e0, e1, e2 = jnp.exp(l0 - top), jnp.exp(l1 - top), jnp.exp(l2 - top)
    inv = 1.0 / (e0 + e1 + e2)
    head_of_col = lax.broadcasted_iota(jnp.int32, (LANES, ATT_GROUP_WIDTH), 1) // ATT_HEAD_DIM
    expand = (head_of_col == lax.broadcasted_iota(jnp.int32, (LANES, ATT_GROUP_WIDTH), 0)).astype(BF16)
    merged = None
    for e, o_ref in ((e0, o0_ref), (e1, o1_ref), (e2, o2_ref)):
        alpha = e * inv
        hi = alpha.astype(BF16)
        lo = (alpha - hi.astype(F32)).astype(BF16)
        wide = _dot(hi, expand) + _dot(lo, expand)
        term = wide * o_ref[...].astype(F32)
        merged = term if merged is None else merged + term
    m = _dot(merged.astype(BF16), w_ref[...])
    out_ref[...] = x_ref[...] + _rms(m, gain_ref[...])


def _att_merge(x, outs, lses, w, gain):
    tokens, d = x.shape
    tm = TOKEN_TILE
    row = lambda i: (i, 0)
    o_spec = pl.BlockSpec((tm, ATT_GROUP_WIDTH), row)
    l_spec = pl.BlockSpec((tm, LANES), row)
    vmem = 4 * tm * d * 4 + 6 * tm * ATT_GROUP_WIDTH * 2 + 6 * tm * LANES * 4 + w.size * 2 + 8 * tm * d * 4 + (8 << 20)
    return pl.pallas_call(
        _att_merge_kernel,
        grid=(tokens // tm,),
        in_specs=[pl.BlockSpec((tm, d), row), o_spec, o_spec, o_spec, l_spec, l_spec, l_spec,
                  _resident(w.shape), _resident((1, d))],
        out_specs=pl.BlockSpec((tm, d), row),
        out_shape=jax.ShapeDtypeStruct((tokens, d), F32),
        compiler_params=_params(1, vmem),
        name="att_merge",
    )(x, *outs, *lses, w, gain)


def _ffn_kernel(x_ref, halo_ref, gpre_ref, wup_ref, cw_ref, cb_ref, wdown_ref, gpost_ref, o_ref, *, tiles_per_seq):
    tm = x_ref.shape[0]
    d_ff = wdown_ref.shape[0]
    x = x_ref[...]
    gpre = gpre_ref[...]
    h = _rms(x, gpre)
    hh = _rms(halo_ref[...], gpre)
    hb_ext = jnp.concatenate([hh, h], axis=0).astype(BF16)
    hb = hb_ext[CONV_HALO:]
    seq_start = (pl.program_id(0) % tiles_per_seq) == 0
    ext_row = lax.broadcasted_iota(jnp.int32, (CONV_HALO + tm, 1), 0)
    keep = jnp.logical_or(ext_row >= CONV_HALO, jnp.logical_not(seq_start))
    acc = jnp.zeros((tm, o_ref.shape[1]), F32)
    for c in range(d_ff // FFN_CHUNK):
        col = c * FFN_CHUNK
        gate = _dot(hb_ext, wup_ref[:, col:col + FFN_CHUNK])
        gate = jnp.where(keep, gate, 0.0)
        val = _dot(hb, wup_ref[:, d_ff + col:d_ff + col + FFN_CHUNK])
        cw = cw_ref[:, col:col + FFN_CHUNK]
        conv = cb_ref[:, col:col + FFN_CHUNK] + cw[2:3] * gate[CONV_HALO:]
        conv = conv + cw[1:2] * gate[CONV_HALO - 1:CONV_HALO - 1 + tm]
        conv = conv + cw[0:1] * gate[CONV_HALO - 2:CONV_HALO - 2 + tm]
        gelu = 0.5 * conv * (1.0 + jnp.tanh(math.sqrt(2.0 / math.pi) * (conv + 0.044715 * (conv * conv * conv))))
        acc = acc + _dot((gelu * val).astype(BF16), wdown_ref[col:col + FFN_CHUNK, :])
    o_ref[...] = x + _rms(acc, gpost_ref[...])


def _ffn(x, gpre, wup, cw, cb, wdown, gpost, seq):
    tokens, d = x.shape
    d_ff = wdown.shape[0]
    tm = TOKEN_TILE
    tps = seq // tm
    halo_blocks = tm // CONV_HALO
    row = lambda i: (i, 0)
    halo_spec = pl.BlockSpec((CONV_HALO, d), lambda i: (jnp.maximum(i * halo_blocks - 1, 0), 0))
    vmem = 4 * tm * d * 4 + (wup.size + wdown.size) * 2 + 6 * tm * d * 4 + 16 * tm * FFN_CHUNK * 4 + (8 << 20)
    return pl.pallas_call(
        functools.partial(_ffn_kernel, tiles_per_seq=tps),
        grid=(tokens // tm,),
        in_specs=[pl.BlockSpec((tm, d), row), halo_spec, _resident((1, d)), _resident(wup.shape),
                  _resident(cw.shape), _resident((1, d_ff)), _resident(wdown.shape), _resident((1, d))],
        out_specs=pl.BlockSpec((tm, d), row),
        out_shape=jax.ShapeDtypeStruct((tokens, d), F32),
        compiler_params=_params(1, vmem),
        name="conv_ffn",
    )(x, x, gpre, wup, cw, cb, wdown, gpost)


def _att_proj_kernel(x_ref, gain_ref, w_ref, cos_ref, pos_ref, neg_ref, o_ref, *, rope_cols):
    hb = _rms(x_ref[...], gain_ref[...]).astype(BF16)
    cos, pos, neg = cos_ref[...], pos_ref[...], neg_ref[...]
    for c in range(w_ref.shape[1] // COL_CHUNK):
        col = c * COL_CHUNK
        acc = _dot(hb, w_ref[:, col:col + COL_CHUNK])
        if col < rope_cols:
            for j in range(COL_CHUNK // LANES):
                t = acc[:, j * LANES:(j + 1) * LANES]
                o_ref[:, col + j * LANES:col + (j + 1) * LANES] = _partial_rotary(t, cos, pos, neg).astype(BF16)
        else:
            o_ref[:, col:col + COL_CHUNK] = acc.astype(BF16)


def _att_proj(x, gain, w, tables, rope_cols, seq):
    tokens, d = x.shape
    n = w.shape[1]
    tm = TOKEN_TILE
    tps = seq // tm
    row = lambda i: (i, 0)
    tab = pl.BlockSpec((tm, LANES), lambda i: (i % tps, 0))
    vmem = 2 * tm * d * 4 + w.size * 2 + 2 * tm * n * 2 + 8 * tm * COL_CHUNK * 4 + (8 << 20)
    return pl.pallas_call(
        functools.partial(_att_proj_kernel, rope_cols=rope_cols),
        grid=(tokens // tm,),
        in_specs=[pl.BlockSpec((tm, d), row), _resident((1, d)), _resident(w.shape), tab, tab, tab],
        out_specs=pl.BlockSpec((tm, n), row),
        out_shape=jax.ShapeDtypeStruct((tokens, n), BF16),
        compiler_params=_params(1, vmem),
        name="att_proj",
    )(x, gain, w, *tables)


def _dil_attn_kernel(q_ref, k_ref, v_ref, o_ref, l_ref):
    length = q_ref.shape[0]
    blk = ATT_BLOCK
    n_blocks = length // blk
    n_pairs = q_ref.shape[1] // LANES
    lane = lax.broadcasted_iota(jnp.int32, (blk, LANES), 1)
    low = lane < ATT_HEAD_DIM
    zero = jnp.zeros((blk, LANES), BF16)

    def valid_mask(n_keys):
        qi = lax.broadcasted_iota(jnp.int32, (2 * blk, n_keys), 0) % blk
        kj = lax.broadcasted_iota(jnp.int32, (2 * blk, n_keys), 1)
        dist = qi + (n_keys - blk) - kj
        return jnp.logical_and(dist >= 0, dist <= blk)

    def block(q_rows, k_rows, n_keys):
        valid = valid_mask(n_keys)
        lse_tile = jnp.zeros((blk, LANES), F32)
        for hp in range(n_pairs):
            cols = slice(hp * LANES, (hp + 1) * LANES)
            q2 = q_ref[q_rows, cols] * (ATT_HEAD_DIM ** -0.5)
            k2 = k_ref[k_rows, cols]
            v2 = v_ref[k_rows, cols]
            qq = jnp.concatenate([jnp.where(low, q2, zero), jnp.where(low, zero, q2)], axis=0)
            s = lax.dot_general(qq, k2, (((1,), (1,)), ((), ())), preferred_element_type=F32)
            s = jnp.where(valid, s, NEG_BIG)
            m = jnp.max(s, axis=-1, keepdims=True)
            p = jnp.exp(s - m)
            den = jnp.sum(p, axis=-1, keepdims=True)
            o = _dot(p.astype(BF16), v2) / den
            o_ref[q_rows, cols] = jnp.where(low, o[:blk], o[blk:]).astype(BF16)
            lse = m + jnp.log(den)
            lse_tile = jnp.where(lane == 2 * hp, lse[:blk], lse_tile)
            lse_tile = jnp.where(lane == 2 * hp + 1, lse[blk:], lse_tile)
        l_ref[q_rows, :] = lse_tile

    block(pl.ds(0, blk), pl.ds(0, blk), blk)

    def later_block(i):
        start = pl.multiple_of(i * blk, blk)
        block(pl.ds(start, blk), pl.ds(start - blk, 2 * blk), 2 * blk)

    if n_blocks > 1:
        def body(i, carry):
            later_block(i)
            return carry
        lax.fori_loop(1, n_blocks, body, 0)


def _dil_attn(q, kv, group, dilation, batch, seq):
    tokens = q.shape[0]
    length = seq // dilation
    gw = ATT_GROUP_WIDTH
    qv = q.reshape(batch, length, dilation * N_GROUPS * gw)
    kvv = kv.reshape(batch, length, dilation * 2 * N_GROUPS * gw)
    blk = lambda f: pl.BlockSpec((None, length, gw), f)
    vmem = 2 * 4 * length * gw * 2 + 2 * length * LANES * 4 + 48 * ATT_BLOCK * 2 * ATT_BLOCK * 4 + (8 << 20)
    o, lse = pl.pallas_call(
        _dil_attn_kernel,
        grid=(batch, dilation),
        in_specs=[blk(lambda b, r: (b, 0, r * N_GROUPS + group)),
                  blk(lambda b, r: (b, 0, r * 2 * N_GROUPS + group)),
                  blk(lambda b, r: (b, 0, r * 2 * N_GROUPS + N_GROUPS + group))],
        out_specs=[blk(lambda b, r: (b, 0, r)),
                   pl.BlockSpec((None, length, LANES), lambda b, r: (b, 0, r))],
        out_shape=[jax.ShapeDtypeStruct((batch, length, dilation * gw), BF16),
                   jax.ShapeDtypeStruct((batch, length, dilation * LANES), F32)],
        compiler_params=_params(2, vmem),
        name=f"dil_attn_d{dilation}",
    )(qv, kvv, kvv)
    return o.reshape(tokens, gw), lse.reshape(tokens, LANES)


def kernel(x, ret_w_in, ret_w_out, ret_gn_gain, kv_norm, att_w_kv, att_w_q, att_w_o, norm_mix_pre, norm_mix_post, norm_ffn_pre, norm_ffn_post, ffn_w_up, ffn_conv_w, ffn_conv_b, ffn_w_down):
    batch, seq, d = x.shape
    depth = norm_mix_pre.shape[0]
    n_ret = ret_w_in.shape[0]
    assert seq % TOKEN_TILE == 0 and seq % RET_CHUNK == 0
    assert all(seq % (dil * ATT_BLOCK) == 0 for _, dil in DIL_GROUPS)
    assert all(win // dil == ATT_BLOCK for win, dil in DIL_GROUPS)

    rcos, rsin, acos, aneg, apos = _rope_tables(seq)
    att_tables = (acos, apos, aneg)
    log_gamma = jnp.log(1.0 - 2.0 ** (-5.0 - jnp.arange(RET_HEADS, dtype=F32)))
    gain = lambda g: g[None, :]
    bf = lambda w: w.astype(BF16)

    xt = x.reshape(batch * seq, d)
    kv = None
    for layer in range(depth):
        if layer < n_ret:
            q, k, v, g = _ret_inproj(xt, gain(norm_mix_pre[layer]), bf(ret_w_in[layer]), rcos, rsin, seq)
            y = _retention(q, k, v, g, gain(ret_gn_gain[layer]), log_gamma, batch, seq)
            xt = _outproj(xt, y, bf(ret_w_out[layer]), gain(norm_mix_post[layer]))
        else:
            bi = layer - n_ret
            q = _att_proj(xt, gain(norm_mix_pre[layer]), bf(att_w_q[bi]), att_tables, att_w_q.shape[2], seq)
            outs, lses = [], []
            for gi, (_, dil) in enumerate(DIL_GROUPS):
                o, lse = _dil_attn(q, kv, gi, dil, batch, seq)
                outs.append(o)
                lses.append(lse)
            xt = _att_merge(xt, outs, lses, bf(att_w_o[bi]), gain(norm_mix_post[layer]))
        xt = _ffn(xt, gain(norm_ffn_pre[layer]), bf(ffn_w_up[layer]), ffn_conv_w[layer], gain(ffn_conv_b[layer]),
                  bf(ffn_w_down[layer]), gain(norm_ffn_post[layer]), seq)
        if layer == n_ret - 1:
            kv = _att_proj(xt, gain(kv_norm), bf(att_w_kv), att_tables, att_w_kv.shape[1] // 2, seq)
    return xt.reshape(batch, seq, d)
```

```python
import functools
import math

import jax
import jax.numpy as jnp
from jax import lax
from jax.experimental import pallas as pl
from jax.experimental.pallas import tpu as pltpu

F32 = jnp.float32
BF16 = jnp.bfloat16

LANES = 128
V7X_VMEM_BYTES = 64 * 1024 * 1024

RMS_EPS = 1e-6
GN_EPS = 1e-6
RET_HEADS = 4
RET_QK_DIM = 256
RET_V_DIM = 512
RET_ROT_BASE = 10000.0
ATT_HEADS = 16
ATT_HEAD_DIM = 64
ATT_GROUP_WIDTH = ATT_HEADS * ATT_HEAD_DIM
DIL_GROUPS = ((128, 1), (512, 4), (2048, 16))
DILATIONS = tuple(d for _, d in DIL_GROUPS)
N_GROUPS = 3
ATT_BLOCK = 128
ROPE_THETA = 500000.0
ROPE_DIMS = ATT_HEAD_DIM // 4
CONV_WIDTH = 3

TOKEN_TILE = 512
COL_CHUNK = 512
FFN_CHUNK = 2816
CONV_HALO = 16
RET_CHUNK = 256
ATT_MIN_BLOCKS = 4
RELAYOUT_SLABS = 8
NEG_BIG = -1e30


def _vmem_limit(nbytes):
    return int(min(V7X_VMEM_BYTES - (4 << 20), max(nbytes, 16 << 20)))


def _params(n_axes, vmem_bytes):
    return pltpu.CompilerParams(dimension_semantics=("arbitrary",) * n_axes,
                                vmem_limit_bytes=_vmem_limit(vmem_bytes))


def _resident(shape):
    return pl.BlockSpec(shape, lambda *_: (0,) * len(shape), pipeline_mode=pl.Buffered(1))


def _rms(x, gain):
    return x * lax.rsqrt(jnp.mean(x * x, axis=-1, keepdims=True) + RMS_EPS) * gain


def _dot(a, b):
    return jnp.dot(a, b, preferred_element_type=F32)


def _dot_nt(a, b):
    return lax.dot_general(a, b, (((1,), (1,)), ((), ())), preferred_element_type=F32)


def _rope_tables_kernel(rfreq_ref, afreq_ref, aneg_ref, apos_ref,
                        rcos_ref, rsin_ref, acos_ref, aneg_out_ref, apos_out_ref):
    seq = rcos_ref.shape[0]
    pos = lax.broadcasted_iota(jnp.int32, (seq, LANES), 0).astype(F32)
    rang = pos * rfreq_ref[...]
    rcos_ref[...] = jnp.cos(rang)
    rsin_ref[...] = jnp.sin(rang)
    aang = pos * afreq_ref[...]
    asin = jnp.sin(aang)
    acos_ref[...] = jnp.cos(aang)
    aneg_out_ref[...] = asin * aneg_ref[...]
    apos_out_ref[...] = asin * apos_ref[...]


def _rope_tables(seq):
    rfreq = 1.0 / (RET_ROT_BASE ** jnp.linspace(0.0, 1.0, RET_QK_DIM // 2, dtype=F32))
    inv = ROPE_THETA ** (-jnp.arange(0, ROPE_DIMS, 2, dtype=F32) / ROPE_DIMS)
    half = ROPE_DIMS // 2
    lane = jnp.arange(LANES) % ATT_HEAD_DIM
    afreq = jnp.where(lane < ROPE_DIMS, inv[lane % half], 0.0).astype(F32)
    aneg = jnp.where(lane < half, -1.0, 0.0).astype(F32)
    apos = jnp.where((lane >= half) & (lane < ROPE_DIMS), 1.0, 0.0).astype(F32)
    tab = jax.ShapeDtypeStruct((seq, LANES), F32)
    return pl.pallas_call(
        _rope_tables_kernel,
        out_shape=(tab,) * 5,
        name="rope_tables",
    )(rfreq[None, :], afreq[None, :], aneg[None, :], apos[None, :])


def _partial_rotary(t, cos, pos, neg):
    return t * cos + pltpu.roll(t, 8, 1) * pos + pltpu.roll(t, LANES - 8, 1) * neg


def _ret_inproj_kernel(x_ref, gain_ref, w_ref, cos_ref, sin_ref, q_ref, k_ref, v_ref, g_ref):
    hb = _rms(x_ref[...], gain_ref[...]).astype(BF16)
    cos = cos_ref[...]
    sin = sin_ref[...]
    dqk = RET_HEADS * RET_QK_DIM
    dv = RET_HEADS * RET_V_DIM
    half = RET_QK_DIM // 2
    n_chunks = w_ref.shape[1] // COL_CHUNK
    for c in range(n_chunks):
        col = c * COL_CHUNK
        acc = _dot(hb, w_ref[:, col:col + COL_CHUNK])
        if col < 2 * dqk:
            dst, base, scale = (q_ref, col, 1.0) if col < dqk else (k_ref, col - dqk, RET_QK_DIM ** -0.5)
            for hh in range(COL_CHUNK // RET_QK_DIM):
                a = acc[:, hh * RET_QK_DIM:hh * RET_QK_DIM + half]
                b = acc[:, hh * RET_QK_DIM + half:(hh + 1) * RET_QK_DIM]
                o = base + hh * RET_QK_DIM
                dst[:, o:o + half] = ((a * cos - b * sin) * scale).astype(BF16)
                dst[:, o + half:o + RET_QK_DIM] = ((b * cos + a * sin) * scale).astype(BF16)
        elif col < 2 * dqk + dv:
            o = col - 2 * dqk
            v_ref[:, o:o + COL_CHUNK] = acc.astype(BF16)
        else:
            o = col - 2 * dqk - dv
            g_ref[:, o:o + COL_CHUNK] = (acc / (1.0 + jnp.exp(-acc))).astype(BF16)


def _ret_inproj(x, gain, w, cos, sin, seq):
    tokens, d = x.shape
    tm = TOKEN_TILE
    tps = seq // tm
    dqk = RET_HEADS * RET_QK_DIM
    dv = RET_HEADS * RET_V_DIM
    row = lambda i: (i, 0)
    tab = pl.BlockSpec((tm, LANES), lambda i: (i % tps, 0))
    vmem = 2 * tm * d * 4 + w.size * 2 + 2 * tm * (2 * dqk + 2 * dv) * 2 + 8 * tm * COL_CHUNK * 4 + (8 << 20)
    return pl.pallas_call(
        _ret_inproj_kernel,
        grid=(tokens // tm,),
        in_specs=[pl.BlockSpec((tm, d), row), _resident((1, d)), _resident(w.shape), tab, tab],
        out_specs=[pl.BlockSpec((tm, dqk), row), pl.BlockSpec((tm, dqk), row),
                   pl.BlockSpec((tm, dv), row), pl.BlockSpec((tm, dv), row)],
        out_shape=[jax.ShapeDtypeStruct((tokens, dqk), BF16), jax.ShapeDtypeStruct((tokens, dqk), BF16),
                   jax.ShapeDtypeStruct((tokens, dv), BF16), jax.ShapeDtypeStruct((tokens, dv), BF16)],
        compiler_params=_params(1, vmem),
        name="ret_inproj",
    )(x, gain, w, cos, sin)


def _retention_kernel(lg_ref, q_ref, k_ref, v_ref, g_ref, gain_ref, o_ref, state_ref):
    head = pl.program_id(1)
    lg = lg_ref[head]
    seq = q_ref.shape[0]
    ck = RET_CHUNK
    n_chunks = seq // ck
    row = lax.broadcasted_iota(jnp.int32, (ck, ck), 0)
    col = lax.broadcasted_iota(jnp.int32, (ck, ck), 1)
    diff = (row - col).astype(F32)
    decay = jnp.where(diff >= 0.0, jnp.exp(jnp.maximum(diff, 0.0) * lg), 0.0)
    n = lax.broadcasted_iota(jnp.int32, (ck, 1), 0).astype(F32)
    xi = jnp.exp((n + 1.0) * lg)
    zeta = jnp.exp((ck - 1.0 - n) * lg)
    chunk_decay = jnp.exp(jnp.full((1, RET_V_DIM), ck, F32) * lg)
    gain = gain_ref[...]

    state_ref[...] = jnp.zeros_like(state_ref)
    for c in range(n_chunks):
        rows = pl.ds(c * ck, ck)
        qc = q_ref[rows, :]
        kc = k_ref[rows, :]
        vc = v_ref[rows, :]
        scores = _dot_nt(qc, kc)
        y = _dot((scores * decay).astype(BF16), vc)
        y = y + _dot(qc, state_ref[...].astype(BF16)) * xi
        if c + 1 < n_chunks:
            kz = (kc.astype(F32) * zeta).astype(BF16)
            upd = lax.dot_general(kz, vc, (((0,), (0,)), ((), ())), preferred_element_type=F32)
            state_ref[...] = state_ref[...] * chunk_decay + upd
        mu = jnp.mean(y, axis=-1, keepdims=True)
        yc = y - mu
        var = jnp.mean(yc * yc, axis=-1, keepdims=True)
        yn = yc * lax.rsqrt(var + GN_EPS) * gain
        o_ref[rows, :] = (g_ref[rows, :].astype(F32) * yn).astype(BF16)


def _retention(q, k, v, g, gn_gain, log_gamma, batch, seq):
    tokens = q.shape[0]
    dv = RET_HEADS * RET_V_DIM
    qk_spec = pl.BlockSpec((seq, RET_QK_DIM), lambda b, h: (b, h))
    v_spec = pl.BlockSpec((seq, RET_V_DIM), lambda b, h: (b, h))
    vmem = 2 * seq * (2 * RET_QK_DIM + 3 * RET_V_DIM) * 2 + 16 * RET_CHUNK * RET_V_DIM * 4 + (8 << 20)
    return pl.pallas_call(
        _retention_kernel,
        grid=(batch, RET_HEADS),
        in_specs=[pl.BlockSpec(memory_space=pltpu.SMEM), qk_spec, qk_spec, v_spec, v_spec,
                  pl.BlockSpec((1, RET_V_DIM), lambda b, h: (0, h))],
        out_specs=v_spec,
        out_shape=jax.ShapeDtypeStruct((tokens, dv), BF16),
        scratch_shapes=[pltpu.VMEM((RET_QK_DIM, RET_V_DIM), F32)],
        compiler_params=_params(2, vmem),
        name="retention",
    )(log_gamma, q, k, v, g, gn_gain)


def _outproj_kernel(x_ref, a_ref, w_ref, gain_ref, o_ref):
    m = _dot(a_ref[...], w_ref[...])
    o_ref[...] = x_ref[...] + _rms(m, gain_ref[...])


def _outproj(x, a, w, gain):
    tokens, d = x.shape
    kdim = a.shape[1]
    tm = TOKEN_TILE
    row = lambda i: (i, 0)
    vmem = 4 * tm * d * 4 + 2 * tm * kdim * 2 + w.size * 2 + 4 * tm * d * 4 + (8 << 20)
    return pl.pallas_call(
        _outproj_kernel,
        grid=(tokens // tm,),
        in_specs=[pl.BlockSpec((tm, d), row), pl.BlockSpec((tm, kdim), row), _resident(w.shape), _resident((1, d))],
        out_specs=pl.BlockSpec((tm, d), row),
        out_shape=jax.ShapeDtypeStruct((tokens, d), F32),
        compiler_params=_params(1, vmem),
        name="outproj",
    )(x, a, w, gain)


def _natural_rows(src_ref, lane0, stage_ref, slab, dilation):
    if dilation == 1:
        return src_ref[0, :, lane0:lane0 + LANES].astype(F32)
    per = src_ref.shape[1]
    for r in range(dilation):
        stage_ref[slab, pl.ds(r, per, stride=dilation), :] = src_ref[r, :, lane0:lane0 + LANES].astype(F32)
    return stage_ref[slab]


def _att_merge_kernel(x_ref, o0_ref, o1_ref, o2_ref, l0_ref, l1_ref, l2_ref, expand_ref, w_ref, gain_ref, out_ref,
                      stage_ref, merged_ref):
    o_refs = (o0_ref, o1_ref, o2_ref)
    lses = [_natural_rows(l_ref, 0, stage_ref, g, DILATIONS[g]) for g, l_ref in enumerate((l0_ref, l1_ref, l2_ref))]
    top = jnp.maximum(jnp.maximum(lses[0], lses[1]), lses[2])
    es = [jnp.exp(l - top) for l in lses]
    inv = 1.0 / (es[0] + es[1] + es[2])
    lane = lax.broadcasted_iota(jnp.int32, top.shape, 1)
    packed = None
    for g, e in enumerate(es):
        alpha = jnp.where(lane < ATT_HEADS, e * inv, 0.0)
        hi = alpha.astype(BF16).astype(F32)
        lo = alpha - hi
        for part, piece in enumerate((hi, lo)):
            shift = (2 * g + part) * ATT_HEADS
            piece = pltpu.roll(piece, shift, 1) if shift else piece
            packed = piece if packed is None else packed + piece
    wide = _dot(packed.astype(BF16), expand_ref[...])
    for j in range(ATT_GROUP_WIDTH // LANES):
        acc = None
        for g in range(N_GROUPS):
            slab = N_GROUPS + g * (ATT_GROUP_WIDTH // LANES) + j
            o = _natural_rows(o_refs[g], j * LANES, stage_ref, slab, DILATIONS[g])
            c0 = g * ATT_GROUP_WIDTH + j * LANES
            term = wide[:, c0:c0 + LANES] * o
            acc = term if acc is None else acc + term
        merged_ref[:, j * LANES:(j + 1) * LANES] = acc.astype(BF16)
    m = _dot(merged_ref[...], w_ref[...])
    out_ref[...] = x_ref[...] + _rms(m, gain_ref[...])


def _att_merge(x, outs, lses, w, gain, seq):
    tokens, d = x.shape
    tm = TOKEN_TILE
    tps = seq // tm
    row = lambda i: (i, 0)
    gw = ATT_GROUP_WIDTH

    def plane(width, dil):
        return pl.BlockSpec((None, dil, tm // dil, width), lambda i: (i // tps, 0, i % tps, 0))

    col = jnp.arange(N_GROUPS * gw)
    src = 2 * ATT_HEADS * (col // gw) + (col % gw) // ATT_HEAD_DIM
    lane = jnp.arange(LANES)[:, None]
    expand = ((lane == src[None, :]) | (lane == src[None, :] + ATT_HEADS)).astype(BF16)
    n_stage = N_GROUPS + N_GROUPS * (gw // LANES)
    vmem = (4 * tm * d * 4 + 6 * tm * gw * 2 + 6 * tm * LANES * 4 + w.size * 2 + expand.size * 2
            + n_stage * tm * LANES * 4 + 8 * tm * d * 4 + (8 << 20))
    return pl.pallas_call(
        _att_merge_kernel,
        grid=(tokens // tm,),
        in_specs=[pl.BlockSpec((tm, d), row)] + [plane(gw, dil) for dil in DILATIONS]
                 + [plane(LANES, dil) for dil in DILATIONS]
                 + [_resident(expand.shape), _resident(w.shape), _resident((1, d))],
        out_specs=pl.BlockSpec((tm, d), row),
        out_shape=jax.ShapeDtypeStruct((tokens, d), F32),
        scratch_shapes=[pltpu.VMEM((n_stage, tm, LANES), F32), pltpu.VMEM((tm, gw), BF16)],
        compiler_params=_params(1, vmem),
        name="att_merge",
    )(x, *outs, *lses, expand, w, gain)


def _ffn_kernel(x_ref, halo_ref, gpre_ref, wup_ref, cw_ref, cb_ref, wdown_ref, gpost_ref, o_ref, *, tiles_per_seq):
    tm = x_ref.shape[0]
    d_ff = wdown_ref.shape[0]
    x = x_ref[...]
    gpre = gpre_ref[...]
    h = _rms(x, gpre)
    hh = _rms(halo_ref[...], gpre)
    hb_ext = jnp.concatenate([hh, h], axis=0).astype(BF16)
    hb = hb_ext[CONV_HALO:]
    seq_start = (pl.program_id(0) % tiles_per_seq) == 0
    ext_row = lax.broadcasted_iota(jnp.int32, (CONV_HALO + tm, 1), 0)
    keep = jnp.logical_or(ext_row >= CONV_HALO, jnp.logical_not(seq_start))
    acc = jnp.zeros((tm, o_ref.shape[1]), F32)
    for col in range(0, d_ff, FFN_CHUNK):
        width = min(FFN_CHUNK, d_ff - col)
        gate = _dot(hb_ext, wup_ref[:, col:col + width])
        gate = jnp.where(keep, gate, 0.0)
        val = _dot(hb, wup_ref[:, d_ff + col:d_ff + col + width])
        cw = cw_ref[:, col:col + width]
        conv = cb_ref[:, col:col + width] + cw[2:3] * gate[CONV_HALO:]
        conv = conv + cw[1:2] * gate[CONV_HALO - 1:CONV_HALO - 1 + tm]
        conv = conv + cw[0:1] * gate[CONV_HALO - 2:CONV_HALO - 2 + tm]
        gelu = 0.5 * conv * (1.0 + jnp.tanh(math.sqrt(2.0 / math.pi) * (conv + 0.044715 * (conv * conv * conv))))
        acc = acc + _dot((gelu * val).astype(BF16), wdown_ref[col:col + width, :])
    o_ref[...] = x + _rms(acc, gpost_ref[...])


def _ffn(x, gpre, wup, cw, cb, wdown, gpost, seq):
    tokens, d = x.shape
    d_ff = wdown.shape[0]
    tm = TOKEN_TILE
    tps = seq // tm
    halo_blocks = tm // CONV_HALO
    row = lambda i: (i, 0)
    halo_spec = pl.BlockSpec((CONV_HALO, d), lambda i: (jnp.maximum(i * halo_blocks - 1, 0), 0))
    vmem = 4 * tm * d * 4 + (wup.size + wdown.size) * 2 + 6 * tm * d * 4 + 16 * tm * FFN_CHUNK * 4 + (8 << 20)
    return pl.pallas_call(
        functools.partial(_ffn_kernel, tiles_per_seq=tps),
        grid=(tokens // tm,),
        in_specs=[pl.BlockSpec((tm, d), row), halo_spec, _resident((1, d)), _resident(wup.shape),
                  _resident(cw.shape), _resident((1, d_ff)), _resident(wdown.shape), _resident((1, d))],
        out_specs=pl.BlockSpec((tm, d), row),
        out_shape=jax.ShapeDtypeStruct((tokens, d), F32),
        compiler_params=_params(1, vmem),
        name="conv_ffn",
    )(x, x, gpre, wup, cw, cb, wdown, gpost)


def _att_proj_kernel(x_ref, gain_ref, w_ref, cos_ref, pos_ref, neg_ref, *refs, dilations, n_rope):
    outs, stage_ref = refs[:-1], refs[-1]
    tm = x_ref.shape[0]
    hb = _rms(x_ref[...], gain_ref[...]).astype(BF16)
    cos, pos, neg = cos_ref[...], pos_ref[...], neg_ref[...]
    slab = 0
    for s, (o_ref, dil) in enumerate(zip(outs, dilations)):
        for c0 in range(0, ATT_GROUP_WIDTH, COL_CHUNK):
            col = s * ATT_GROUP_WIDTH + c0
            acc = _dot(hb, w_ref[:, col:col + COL_CHUNK])
            for j in range(COL_CHUNK // LANES):
                t = acc[:, j * LANES:(j + 1) * LANES]
                if s < n_rope:
                    t = _partial_rotary(t, cos, pos, neg)
                lane0 = c0 + j * LANES
                if dil == 1:
                    o_ref[0, :, lane0:lane0 + LANES] = t.astype(BF16)
                else:
                    stage_ref[slab] = t
                    for r in range(dil):
                        piece = stage_ref[slab, pl.ds(r, tm // dil, stride=dil), :]
                        o_ref[r, :, lane0:lane0 + LANES] = piece.astype(BF16)
                    slab = (slab + 1) % RELAYOUT_SLABS


def _att_proj(x, gain, w, tables, n_rope, batch, seq):
    tokens, d = x.shape
    n = w.shape[1]
    gw = ATT_GROUP_WIDTH
    dilations = DILATIONS * (n // (N_GROUPS * gw))
    tm = TOKEN_TILE
    tps = seq // tm
    row = lambda i: (i, 0)
    tab = pl.BlockSpec((tm, LANES), lambda i: (i % tps, 0))
    plane = lambda dil: pl.BlockSpec((None, dil, tm // dil, gw), lambda i: (i // tps, 0, i % tps, 0))
    vmem = (2 * tm * d * 4 + w.size * 2 + 2 * tm * n * 2 + 8 * tm * COL_CHUNK * 4
            + RELAYOUT_SLABS * tm * LANES * 4 + (8 << 20))
    return pl.pallas_call(
        functools.partial(_att_proj_kernel, dilations=dilations, n_rope=n_rope),
        grid=(tokens // tm,),
        in_specs=[pl.BlockSpec((tm, d), row), _resident((1, d)), _resident(w.shape), tab, tab, tab],
        out_specs=[plane(dil) for dil in dilations],
        out_shape=[jax.ShapeDtypeStruct((batch, dil, seq // dil, gw), BF16) for dil in dilations],
        scratch_shapes=[pltpu.VMEM((RELAYOUT_SLABS, tm, LANES), F32)],
        compiler_params=_params(1, vmem),
        name="att_proj",
    )(x, gain, w, *tables)


def _dil_attn_kernel(q_ref, k_ref, v_ref, o_ref, l_ref, bias_first_ref, bias_band_ref):
    n_planes, length = q_ref.shape[0], q_ref.shape[1]
    blk = ATT_BLOCK
    n_blocks = length // blk
    n_pairs = q_ref.shape[2] // LANES
    lane = lax.broadcasted_iota(jnp.int32, (blk, LANES), 1)
    low = lane < ATT_HEAD_DIM
    zero = jnp.zeros((blk, LANES), BF16)

    def bias(n_keys):
        qi = lax.broadcasted_iota(jnp.int32, (2 * blk, n_keys), 0) % blk
        kj = lax.broadcasted_iota(jnp.int32, (2 * blk, n_keys), 1)
        dist = qi + (n_keys - blk) - kj
        return jnp.where(jnp.logical_and(dist >= 0, dist <= blk), 0.0, NEG_BIG).astype(F32)

    bias_first_ref[...] = bias(blk)
    bias_band_ref[...] = bias(2 * blk)

    def block(plane, q_rows, k_rows, bias_ref):
        n_keys = bias_ref.shape[1]
        ones = jnp.ones((n_keys, LANES), BF16)
        m_tile = jnp.zeros((blk, LANES), F32)
        d_tile = jnp.ones((blk, LANES), F32)
        for hp in range(n_pairs):
            cols = slice(hp * LANES, (hp + 1) * LANES)
            q2 = q_ref[plane, q_rows, cols] * (ATT_HEAD_DIM ** -0.5)
            k2 = k_ref[plane, k_rows, cols]
            v2 = v_ref[plane, k_rows, cols]
            qq = jnp.concatenate([jnp.where(low, q2, zero), jnp.where(low, zero, q2)], axis=0)
            s = _dot_nt(qq, k2) + bias_ref[...]
            m = jnp.max(s, axis=-1, keepdims=True)
            p = jnp.exp(s - m).astype(BF16)
            pv = _dot(p, jnp.concatenate([v2, ones], axis=1))
            den = pv[:, LANES:]
            o = pv[:, :LANES] / den
            o_ref[plane, q_rows, cols] = jnp.where(low, o[:blk], o[blk:]).astype(BF16)
            m_tile = jnp.where(lane == 2 * hp, m[:blk], jnp.where(lane == 2 * hp + 1, m[blk:], m_tile))
            d_tile = jnp.where(lane == 2 * hp, den[:blk], jnp.where(lane == 2 * hp + 1, den[blk:], d_tile))
        l_ref[plane, q_rows, :] = m_tile + jnp.log(d_tile)

    for plane in range(n_planes):
        block(plane, pl.ds(0, blk), pl.ds(0, blk), bias_first_ref)

        def later_block(i, carry, plane=plane):
            start = pl.multiple_of(i * blk, blk)
            block(plane, pl.ds(start, blk), pl.ds(start - blk, 2 * blk), bias_band_ref)
            return carry

        if n_blocks > 1:
            lax.fori_loop(1, n_blocks, later_block, 0)


def _dil_attn(q, k, v):
    batch, dil, length, gw = q.shape
    planes = min(dil, max(1, ATT_MIN_BLOCKS * ATT_BLOCK // length))
    spec = lambda width: pl.BlockSpec((None, planes, length, width), lambda b, r: (b, r, 0, 0))
    blk = ATT_BLOCK
    vmem = (2 * 4 * planes * length * gw * 2 + 2 * planes * length * LANES * 4 + 3 * 2 * blk * 2 * blk * 4
            + 48 * blk * 2 * blk * 4 + (8 << 20))
    return pl.pallas_call(
        _dil_attn_kernel,
        grid=(batch, dil // planes),
        in_specs=[spec(gw), spec(gw), spec(gw)],
        out_specs=[spec(gw), spec(LANES)],
        out_shape=[jax.ShapeDtypeStruct((batch, dil, length, gw), BF16),
                   jax.ShapeDtypeStruct((batch, dil, length, LANES), F32)],
        scratch_shapes=[pltpu.VMEM((2 * blk, blk), F32), pltpu.VMEM((2 * blk, 2 * blk), F32)],
        compiler_params=_params(2, vmem),
        name=f"dil_attn_d{dil}",
    )(q, k, v)


def kernel(x, ret_w_in, ret_w_out, ret_gn_gain, kv_norm, att_w_kv, att_w_q, att_w_o, norm_mix_pre, norm_mix_post, norm_ffn_pre, norm_ffn_post, ffn_w_up, ffn_conv_w, ffn_conv_b, ffn_w_down):
    batch, seq, d = x.shape
    depth = norm_mix_pre.shape[0]
    n_ret = ret_w_in.shape[0]
    assert seq % TOKEN_TILE == 0 and seq % RET_CHUNK == 0
    assert all(seq % (dil * ATT_BLOCK) == 0 and TOKEN_TILE % (16 * dil) == 0 for dil in DILATIONS)
    assert all(win // dil == ATT_BLOCK for win, dil in DIL_GROUPS)

    rcos, rsin, acos, aneg, apos = _rope_tables(seq)
    att_tables = (acos, apos, aneg)
    log_gamma = jnp.log(1.0 - 2.0 ** (-5.0 - jnp.arange(RET_HEADS, dtype=F32)))
    gain = lambda g: g[None, :]
    bf = lambda w: w.astype(BF16)

    xt = x.reshape(batch * seq, d)
    keys = values = None
    for layer in range(depth):
        if layer < n_ret:
            q, k, v, g = _ret_inproj(xt, gain(norm_mix_pre[layer]), bf(ret_w_in[layer]), rcos, rsin, seq)
            y = _retention(q, k, v, g, gain(ret_gn_gain[layer]), log_gamma, batch, seq)
            xt = _outproj(xt, y, bf(ret_w_out[layer]), gain(norm_mix_post[layer]))
        else:
            bi = layer - n_ret
            queries = _att_proj(xt, gain(norm_mix_pre[layer]), bf(att_w_q[bi]), att_tables, N_GROUPS, batch, seq)
            outs, lses = zip(*[_dil_attn(queries[gi], keys[gi], values[gi]) for gi in range(N_GROUPS)])
            xt = _att_merge(xt, outs, lses, bf(att_w_o[bi]), gain(norm_mix_post[layer]), seq)
        xt = _ffn(xt, gain(norm_ffn_pre[layer]), bf(ffn_w_up[layer]), ffn_conv_w[layer], gain(ffn_conv_b[layer]),
                  bf(ffn_w_down[layer]), gain(norm_ffn_post[layer]), seq)
        if layer == n_ret - 1:
            kv = _att_proj(xt, gain(kv_norm), bf(att_w_kv), att_tables, N_GROUPS, batch, seq)
            keys, values = kv[:N_GROUPS], kv[N_GROUPS:]
    return xt.reshape(batch, seq, d)
```

```python
import functools
import math

import jax
import jax.numpy as jnp
from jax import lax
from jax.experimental import pallas as pl
from jax.experimental.pallas import tpu as pltpu

F32 = jnp.float32
BF16 = jnp.bfloat16

LANES = 128
V7X_VMEM_BYTES = 64 * 1024 * 1024

RMS_EPS = 1e-6
GN_EPS = 1e-6
RET_HEADS = 4
RET_QK_DIM = 256
RET_V_DIM = 512
RET_ROT_BASE = 10000.0
ATT_HEADS = 16
ATT_HEAD_DIM = 64
ATT_GROUP_WIDTH = ATT_HEADS * ATT_HEAD_DIM
DIL_GROUPS = ((128, 1), (512, 4), (2048, 16))
DILATIONS = tuple(d for _, d in DIL_GROUPS)
N_GROUPS = 3
ATT_BLOCK = 128
ROPE_THETA = 500000.0
ROPE_DIMS = ATT_HEAD_DIM // 4
CONV_WIDTH = 3

TOKEN_TILE = 512
COL_CHUNK = 512
FFN_CHUNK = 2816
CONV_HALO = 16
RET_CHUNK = 256
ATT_MIN_BLOCKS = 4
ATT_SKEW = 4
ATT_UNROLL = 5
RELAYOUT_SLABS = 8
NEG_BIG = -1e30


def _vmem_limit(nbytes):
    return int(min(V7X_VMEM_BYTES - (4 << 20), max(nbytes, 16 << 20)))


def _params(n_axes, vmem_bytes):
    return pltpu.CompilerParams(dimension_semantics=("arbitrary",) * n_axes,
                                vmem_limit_bytes=_vmem_limit(vmem_bytes))


def _resident(shape):
    return pl.BlockSpec(shape, lambda *_: (0,) * len(shape), pipeline_mode=pl.Buffered(1))


def _rms(x, gain):
    return x * lax.rsqrt(jnp.mean(x * x, axis=-1, keepdims=True) + RMS_EPS) * gain


def _dot(a, b):
    return jnp.dot(a, b, preferred_element_type=F32)


def _dot_nt(a, b):
    return lax.dot_general(a, b, (((1,), (1,)), ((), ())), preferred_element_type=F32)


def _rope_tables_kernel(rfreq_ref, afreq_ref, aneg_ref, apos_ref,
                        rcos_ref, rsin_ref, acos_ref, aneg_out_ref, apos_out_ref):
    seq = rcos_ref.shape[0]
    pos = lax.broadcasted_iota(jnp.int32, (seq, LANES), 0).astype(F32)
    rang = pos * rfreq_ref[...]
    rcos_ref[...] = jnp.cos(rang)
    rsin_ref[...] = jnp.sin(rang)
    aang = pos * afreq_ref[...]
    asin = jnp.sin(aang)
    acos_ref[...] = jnp.cos(aang)
    aneg_out_ref[...] = asin * aneg_ref[...]
    apos_out_ref[...] = asin * apos_ref[...]


def _rope_tables(seq):
    rfreq = 1.0 / (RET_ROT_BASE ** jnp.linspace(0.0, 1.0, RET_QK_DIM // 2, dtype=F32))
    inv = ROPE_THETA ** (-jnp.arange(0, ROPE_DIMS, 2, dtype=F32) / ROPE_DIMS)
    half = ROPE_DIMS // 2
    lane = jnp.arange(LANES) % ATT_HEAD_DIM
    afreq = jnp.where(lane < ROPE_DIMS, inv[lane % half], 0.0).astype(F32)
    aneg = jnp.where(lane < half, -1.0, 0.0).astype(F32)
    apos = jnp.where((lane >= half) & (lane < ROPE_DIMS), 1.0, 0.0).astype(F32)
    tab = jax.ShapeDtypeStruct((seq, LANES), F32)
    return pl.pallas_call(
        _rope_tables_kernel,
        out_shape=(tab,) * 5,
        name="rope_tables",
    )(rfreq[None, :], afreq[None, :], aneg[None, :], apos[None, :])


def _partial_rotary(t, cos, pos, neg):
    return t * cos + pltpu.roll(t, 8, 1) * pos + pltpu.roll(t, LANES - 8, 1) * neg


def _ret_inproj_kernel(x_ref, gain_ref, w_ref, cos_ref, sin_ref, q_ref, k_ref, v_ref, g_ref):
    hb = _rms(x_ref[...], gain_ref[...]).astype(BF16)
    cos = cos_ref[...]
    sin = sin_ref[...]
    dqk = RET_HEADS * RET_QK_DIM
    dv = RET_HEADS * RET_V_DIM
    half = RET_QK_DIM // 2
    n_chunks = w_ref.shape[1] // COL_CHUNK
    for c in range(n_chunks):
        col = c * COL_CHUNK
        acc = _dot(hb, w_ref[:, col:col + COL_CHUNK])
        if col < 2 * dqk:
            dst, base, scale = (q_ref, col, 1.0) if col < dqk else (k_ref, col - dqk, RET_QK_DIM ** -0.5)
            for hh in range(COL_CHUNK // RET_QK_DIM):
                a = acc[:, hh * RET_QK_DIM:hh * RET_QK_DIM + half]
                b = acc[:, hh * RET_QK_DIM + half:(hh + 1) * RET_QK_DIM]
                o = base + hh * RET_QK_DIM
                dst[:, o:o + half] = ((a * cos - b * sin) * scale).astype(BF16)
                dst[:, o + half:o + RET_QK_DIM] = ((b * cos + a * sin) * scale).astype(BF16)
        elif col < 2 * dqk + dv:
            o = col - 2 * dqk
            v_ref[:, o:o + COL_CHUNK] = acc.astype(BF16)
        else:
            o = col - 2 * dqk - dv
            g_ref[:, o:o + COL_CHUNK] = (acc / (1.0 + jnp.exp(-acc))).astype(BF16)


def _ret_inproj(x, gain, w, cos, sin, seq):
    tokens, d = x.shape
    tm = TOKEN_TILE
    tps = seq // tm
    dqk = RET_HEADS * RET_QK_DIM
    dv = RET_HEADS * RET_V_DIM
    row = lambda i: (i, 0)
    tab = pl.BlockSpec((tm, LANES), lambda i: (i % tps, 0))
    vmem = 2 * tm * d * 4 + w.size * 2 + 2 * tm * (2 * dqk + 2 * dv) * 2 + 8 * tm * COL_CHUNK * 4 + (8 << 20)
    return pl.pallas_call(
        _ret_inproj_kernel,
        grid=(tokens // tm,),
        in_specs=[pl.BlockSpec((tm, d), row), _resident((1, d)), _resident(w.shape), tab, tab],
        out_specs=[pl.BlockSpec((tm, dqk), row), pl.BlockSpec((tm, dqk), row),
                   pl.BlockSpec((tm, dv), row), pl.BlockSpec((tm, dv), row)],
        out_shape=[jax.ShapeDtypeStruct((tokens, dqk), BF16), jax.ShapeDtypeStruct((tokens, dqk), BF16),
                   jax.ShapeDtypeStruct((tokens, dv), BF16), jax.ShapeDtypeStruct((tokens, dv), BF16)],
        compiler_params=_params(1, vmem),
        name="ret_inproj",
    )(x, gain, w, cos, sin)


def _retention_kernel(lg_ref, q_ref, k_ref, v_ref, g_ref, gain_ref, o_ref, state_ref):
    head = pl.program_id(1)
    lg = lg_ref[head]
    seq = q_ref.shape[0]
    ck = RET_CHUNK
    n_chunks = seq // ck
    row = lax.broadcasted_iota(jnp.int32, (ck, ck), 0)
    col = lax.broadcasted_iota(jnp.int32, (ck, ck), 1)
    diff = (row - col).astype(F32)
    decay = jnp.where(diff >= 0.0, jnp.exp(jnp.maximum(diff, 0.0) * lg), 0.0)
    n = lax.broadcasted_iota(jnp.int32, (ck, 1), 0).astype(F32)
    xi = jnp.exp((n + 1.0) * lg)
    zeta = jnp.exp((ck - 1.0 - n) * lg)
    chunk_decay = jnp.exp(jnp.full((1, RET_V_DIM), ck, F32) * lg)
    gain = gain_ref[...]

    state_ref[...] = jnp.zeros_like(state_ref)
    for c in range(n_chunks):
        rows = pl.ds(c * ck, ck)
        qc = q_ref[rows, :]
        kc = k_ref[rows, :]
        vc = v_ref[rows, :]
        scores = _dot_nt(qc, kc)
        y = _dot((scores * decay).astype(BF16), vc)
        y = y + _dot(qc, state_ref[...].astype(BF16)) * xi
        if c + 1 < n_chunks:
            kz = (kc.astype(F32) * zeta).astype(BF16)
            upd = lax.dot_general(kz, vc, (((0,), (0,)), ((), ())), preferred_element_type=F32)
            state_ref[...] = state_ref[...] * chunk_decay + upd
        mu = jnp.mean(y, axis=-1, keepdims=True)
        yc = y - mu
        var = jnp.mean(yc * yc, axis=-1, keepdims=True)
        yn = yc * lax.rsqrt(var + GN_EPS) * gain
        o_ref[rows, :] = (g_ref[rows, :].astype(F32) * yn).astype(BF16)


def _retention(q, k, v, g, gn_gain, log_gamma, batch, seq):
    tokens = q.shape[0]
    dv = RET_HEADS * RET_V_DIM
    qk_spec = pl.BlockSpec((seq, RET_QK_DIM), lambda b, h: (b, h))
    v_spec = pl.BlockSpec((seq, RET_V_DIM), lambda b, h: (b, h))
    vmem = 2 * seq * (2 * RET_QK_DIM + 3 * RET_V_DIM) * 2 + 16 * RET_CHUNK * RET_V_DIM * 4 + (8 << 20)
    return pl.pallas_call(
        _retention_kernel,
        grid=(batch, RET_HEADS),
        in_specs=[pl.BlockSpec(memory_space=pltpu.SMEM), qk_spec, qk_spec, v_spec, v_spec,
                  pl.BlockSpec((1, RET_V_DIM), lambda b, h: (0, h))],
        out_specs=v_spec,
        out_shape=jax.ShapeDtypeStruct((tokens, dv), BF16),
        scratch_shapes=[pltpu.VMEM((RET_QK_DIM, RET_V_DIM), F32)],
        compiler_params=_params(2, vmem),
        name="retention",
    )(log_gamma, q, k, v, g, gn_gain)


def _outproj_kernel(x_ref, a_ref, w_ref, gain_ref, o_ref):
    m = _dot(a_ref[...], w_ref[...])
    o_ref[...] = x_ref[...] + _rms(m, gain_ref[...])


def _outproj(x, a, w, gain):
    tokens, d = x.shape
    kdim = a.shape[1]
    tm = TOKEN_TILE
    row = lambda i: (i, 0)
    vmem = 4 * tm * d * 4 + 2 * tm * kdim * 2 + w.size * 2 + 4 * tm * d * 4 + (8 << 20)
    return pl.pallas_call(
        _outproj_kernel,
        grid=(tokens // tm,),
        in_specs=[pl.BlockSpec((tm, d), row), pl.BlockSpec((tm, kdim), row), _resident(w.shape), _resident((1, d))],
        out_specs=pl.BlockSpec((tm, d), row),
        out_shape=jax.ShapeDtypeStruct((tokens, d), F32),
        compiler_params=_params(1, vmem),
        name="outproj",
    )(x, a, w, gain)


def _natural_rows(src_ref, lane0, stage_ref, slab, dilation):
    if dilation == 1:
        return src_ref[0, :, lane0:lane0 + LANES].astype(F32)
    per = src_ref.shape[1]
    for r in range(dilation):
        stage_ref[slab, pl.ds(r, per, stride=dilation), :] = src_ref[r, :, lane0:lane0 + LANES].astype(F32)
    return stage_ref[slab]


def _att_merge_kernel(x_ref, o0_ref, o1_ref, o2_ref, m0_ref, m1_ref, m2_ref, d0_ref, d1_ref, d2_ref,
                      expand_ref, w_ref, gain_ref, out_ref, stage_ref, merged_ref):
    o_refs = (o0_ref, o1_ref, o2_ref)
    ms = [_natural_rows(r, 0, stage_ref, g, DILATIONS[g]) for g, r in enumerate((m0_ref, m1_ref, m2_ref))]
    dens = [_natural_rows(r, 0, stage_ref, N_GROUPS + g, DILATIONS[g]) for g, r in enumerate((d0_ref, d1_ref, d2_ref))]
    top = jnp.maximum(jnp.maximum(ms[0], ms[1]), ms[2])
    es = [jnp.exp2(m - top) for m in ms]
    inv = 1.0 / (es[0] * dens[0] + es[1] * dens[1] + es[2] * dens[2])
    lane = lax.broadcasted_iota(jnp.int32, top.shape, 1)
    packed = None
    for g, e in enumerate(es):
        alpha = jnp.where(lane < ATT_HEADS, e * inv, 0.0)
        hi = alpha.astype(BF16).astype(F32)
        lo = alpha - hi
        for part, piece in enumerate((hi, lo)):
            shift = (2 * g + part) * ATT_HEADS
            piece = pltpu.roll(piece, shift, 1) if shift else piece
            packed = piece if packed is None else packed + piece
    wide = _dot(packed.astype(BF16), expand_ref[...])
    for j in range(ATT_GROUP_WIDTH // LANES):
        acc = None
        for g in range(N_GROUPS):
            slab = 2 * N_GROUPS + g * (ATT_GROUP_WIDTH // LANES) + j
            o = _natural_rows(o_refs[g], j * LANES, stage_ref, slab, DILATIONS[g])
            c0 = g * ATT_GROUP_WIDTH + j * LANES
            term = wide[:, c0:c0 + LANES] * o
            acc = term if acc is None else acc + term
        merged_ref[:, j * LANES:(j + 1) * LANES] = acc.astype(BF16)
    m = _dot(merged_ref[...], w_ref[...])
    out_ref[...] = x_ref[...] + _rms(m, gain_ref[...])


def _att_merge(x, outs, maxes, dens, w, gain, seq):
    tokens, d = x.shape
    tm = TOKEN_TILE
    tps = seq // tm
    row = lambda i: (i, 0)
    gw = ATT_GROUP_WIDTH

    def plane(width, dil):
        return pl.BlockSpec((None, dil, tm // dil, width), lambda i: (i // tps, 0, i % tps, 0))

    col = jnp.arange(N_GROUPS * gw)
    src = 2 * ATT_HEADS * (col // gw) + (col % gw) // ATT_HEAD_DIM
    lane = jnp.arange(LANES)[:, None]
    expand = ((lane == src[None, :]) | (lane == src[None, :] + ATT_HEADS)).astype(BF16)
    n_stage = 2 * N_GROUPS + N_GROUPS * (gw // LANES)
    vmem = (4 * tm * d * 4 + 6 * tm * gw * 2 + 12 * tm * LANES * 4 + w.size * 2 + expand.size * 2
            + n_stage * tm * LANES * 4 + 8 * tm * d * 4 + (8 << 20))
    return pl.pallas_call(
        _att_merge_kernel,
        grid=(tokens // tm,),
        in_specs=[pl.BlockSpec((tm, d), row)] + [plane(gw, dil) for dil in DILATIONS]
                 + [plane(LANES, dil) for dil in DILATIONS] * 2
                 + [_resident(expand.shape), _resident(w.shape), _resident((1, d))],
        out_specs=pl.BlockSpec((tm, d), row),
        out_shape=jax.ShapeDtypeStruct((tokens, d), F32),
        scratch_shapes=[pltpu.VMEM((n_stage, tm, LANES), F32), pltpu.VMEM((tm, gw), BF16)],
        compiler_params=_params(1, vmem),
        name="att_merge",
    )(x, *outs, *maxes, *dens, expand, w, gain)


def _ffn_kernel(x_ref, halo_ref, gpre_ref, wup_ref, cw_ref, cb_ref, wdown_ref, gpost_ref, o_ref, *, tiles_per_seq):
    tm = x_ref.shape[0]
    d_ff = wdown_ref.shape[0]
    x = x_ref[...]
    gpre = gpre_ref[...]
    h = _rms(x, gpre)
    hh = _rms(halo_ref[...], gpre)
    hb_ext = jnp.concatenate([hh, h], axis=0).astype(BF16)
    hb = hb_ext[CONV_HALO:]
    seq_start = (pl.program_id(0) % tiles_per_seq) == 0
    ext_row = lax.broadcasted_iota(jnp.int32, (CONV_HALO + tm, 1), 0)
    keep = jnp.logical_or(ext_row >= CONV_HALO, jnp.logical_not(seq_start))
    acc = jnp.zeros((tm, o_ref.shape[1]), F32)
    for col in range(0, d_ff, FFN_CHUNK):
        width = min(FFN_CHUNK, d_ff - col)
        gate = _dot(hb_ext, wup_ref[:, col:col + width])
        gate = jnp.where(keep, gate, 0.0)
        val = _dot(hb, wup_ref[:, d_ff + col:d_ff + col + width])
        cw = cw_ref[:, col:col + width]
        conv = cb_ref[:, col:col + width] + cw[2:3] * gate[CONV_HALO:]
        conv = conv + cw[1:2] * gate[CONV_HALO - 1:CONV_HALO - 1 + tm]
        conv = conv + cw[0:1] * gate[CONV_HALO - 2:CONV_HALO - 2 + tm]
        gelu = 0.5 * conv * (1.0 + jnp.tanh(math.sqrt(2.0 / math.pi) * (conv + 0.044715 * (conv * conv * conv))))
        acc = acc + _dot((gelu * val).astype(BF16), wdown_ref[col:col + width, :])
    o_ref[...] = x + _rms(acc, gpost_ref[...])


def _ffn(x, gpre, wup, cw, cb, wdown, gpost, seq):
    tokens, d = x.shape
    d_ff = wdown.shape[0]
    tm = TOKEN_TILE
    tps = seq // tm
    halo_blocks = tm // CONV_HALO
    row = lambda i: (i, 0)
    halo_spec = pl.BlockSpec((CONV_HALO, d), lambda i: (jnp.maximum(i * halo_blocks - 1, 0), 0))
    vmem = 4 * tm * d * 4 + (wup.size + wdown.size) * 2 + 6 * tm * d * 4 + 16 * tm * FFN_CHUNK * 4 + (8 << 20)
    return pl.pallas_call(
        functools.partial(_ffn_kernel, tiles_per_seq=tps),
        grid=(tokens // tm,),
        in_specs=[pl.BlockSpec((tm, d), row), halo_spec, _resident((1, d)), _resident(wup.shape),
                  _resident(cw.shape), _resident((1, d_ff)), _resident(wdown.shape), _resident((1, d))],
        out_specs=pl.BlockSpec((tm, d), row),
        out_shape=jax.ShapeDtypeStruct((tokens, d), F32),
        compiler_params=_params(1, vmem),
        name="conv_ffn",
    )(x, x, gpre, wup, cw, cb, wdown, gpost)


def _att_proj_kernel(x_ref, gain_ref, w_ref, cos_ref, pos_ref, neg_ref, *refs, dilations, n_rope, scale):
    outs, stage_ref = refs[:-1], refs[-1]
    tm = x_ref.shape[0]
    hb = _rms(x_ref[...], gain_ref[...]).astype(BF16)
    cos, pos, neg = cos_ref[...], pos_ref[...], neg_ref[...]
    slab = 0
    for s in sorted(range(len(outs)), key=lambda i: -dilations[i]):
        o_ref, dil = outs[s], dilations[s]
        for c0 in range(0, ATT_GROUP_WIDTH, COL_CHUNK):
            col = s * ATT_GROUP_WIDTH + c0
            acc = _dot(hb, w_ref[:, col:col + COL_CHUNK])
            for j in range(COL_CHUNK // LANES):
                t = acc[:, j * LANES:(j + 1) * LANES]
                if s < n_rope:
                    t = _partial_rotary(t, cos, pos, neg)
                if scale != 1.0:
                    t = t * scale
                lane0 = c0 + j * LANES
                if dil == 1:
                    o_ref[0, :, lane0:lane0 + LANES] = t.astype(BF16)
                else:
                    stage_ref[slab] = t
                    for r in range(dil):
                        piece = stage_ref[slab, pl.ds(r, tm // dil, stride=dil), :]
                        o_ref[r, :, lane0:lane0 + LANES] = piece.astype(BF16)
                    slab = (slab + 1) % RELAYOUT_SLABS


def _att_proj(x, gain, w, tables, n_rope, batch, seq, scale=1.0):
    tokens, d = x.shape
    n = w.shape[1]
    gw = ATT_GROUP_WIDTH
    dilations = DILATIONS * (n // (N_GROUPS * gw))
    tm = TOKEN_TILE
    tps = seq // tm
    row = lambda i: (i, 0)
    tab = pl.BlockSpec((tm, LANES), lambda i: (i % tps, 0))
    plane = lambda dil: pl.BlockSpec((None, dil, tm // dil, gw), lambda i: (i // tps, 0, i % tps, 0))
    vmem = (2 * tm * d * 4 + w.size * 2 + 2 * tm * n * 2 + 8 * tm * COL_CHUNK * 4
            + RELAYOUT_SLABS * tm * LANES * 4 + (8 << 20))
    return pl.pallas_call(
        functools.partial(_att_proj_kernel, dilations=dilations, n_rope=n_rope, scale=scale),
        grid=(tokens // tm,),
        in_specs=[pl.BlockSpec((tm, d), row), _resident((1, d)), _resident(w.shape), tab, tab, tab],
        out_specs=[plane(dil) for dil in dilations],
        out_shape=[jax.ShapeDtypeStruct((batch, dil, seq // dil, gw), BF16) for dil in dilations],
        scratch_shapes=[pltpu.VMEM((RELAYOUT_SLABS, tm, LANES), F32)],
        compiler_params=_params(1, vmem),
        name="att_proj",
    )(x, gain, w, *tables)


def _dil_attn_kernel(q_ref, k_ref, v_ref, o_ref, m_ref, d_ref, bias_first_ref, bias_band_ref):
    n_planes, length = q_ref.shape[0], q_ref.shape[1]
    blk = ATT_BLOCK
    n_blocks = length // blk
    n_pairs = q_ref.shape[2] // LANES
    lane = lax.broadcasted_iota(jnp.int32, (blk, LANES), 1)
    low = lane < ATT_HEAD_DIM
    zero = jnp.zeros((blk, LANES), BF16)

    def bias(n_keys):
        qi = lax.broadcasted_iota(jnp.int32, (2 * blk, n_keys), 0) % blk
        kj = lax.broadcasted_iota(jnp.int32, (2 * blk, n_keys), 1)
        dist = qi + (n_keys - blk) - kj
        return jnp.where(jnp.logical_and(dist >= 0, dist <= blk), 0.0, NEG_BIG).astype(F32)

    bias_first_ref[...] = bias(blk)
    bias_band_ref[...] = bias(2 * blk)

    def scores(plane, q_rows, k_rows, bias_ref, hp):
        cols = slice(hp * LANES, (hp + 1) * LANES)
        q2 = q_ref[plane, q_rows, cols]
        qq = jnp.concatenate([jnp.where(low, q2, zero), jnp.where(low, zero, q2)], axis=0)
        return _dot_nt(qq, k_ref[plane, k_rows, cols]) + bias_ref[...]

    def softmax(s):
        m = jnp.max(s, axis=-1, keepdims=True)
        return m, jnp.exp2(s - m).astype(BF16)

    def finish(plane, q_rows, k_rows, bias_ref, hp, mp):
        m, p = mp
        cols = slice(hp * LANES, (hp + 1) * LANES)
        ones = jnp.ones((bias_ref.shape[1], LANES), BF16)
        pv = _dot(p, jnp.concatenate([v_ref[plane, k_rows, cols], ones], axis=1))
        o_ref[plane, q_rows, cols] = jnp.where(low, pv[:blk, :LANES], pv[blk:, :LANES]).astype(BF16)
        for half in range(2):
            head = 2 * hp + half
            rows = slice(half * blk, (half + 1) * blk)
            m_ref[plane, q_rows, head:head + 1] = m[rows]
            d_ref[plane, q_rows, head:head + 1] = pv[rows, LANES + head:LANES + head + 1]

    def run(items):
        work = [(*item, hp) for item in items for hp in range(n_pairs)]
        for plane, q_rows, _, _ in items:
            m_ref[plane, q_rows, :] = jnp.zeros((blk, LANES), F32)
            d_ref[plane, q_rows, :] = jnp.ones((blk, LANES), F32)
        n = len(work)
        s_vals, mp_vals = {}, {}
        for t in range(-ATT_SKEW, n):
            if t + ATT_SKEW < n:
                s_vals[t + ATT_SKEW] = scores(*work[t + ATT_SKEW])
            mid = t + ATT_SKEW // 2
            if 0 <= mid < n and ATT_SKEW > 1:
                mp_vals[mid] = softmax(s_vals.pop(mid))
            if t >= 0:
                if ATT_SKEW <= 1:
                    mp_vals[t] = softmax(s_vals.pop(t))
                finish(*work[t], mp_vals.pop(t))

    run([(plane, pl.ds(0, blk), pl.ds(0, blk), bias_first_ref) for plane in range(n_planes)])
    n_band = n_blocks - 1
    unroll = max(u for u in range(1, ATT_UNROLL + 1) if n_band % u == 0) if n_band else 0
    for plane in range(n_planes if n_band else 0):
        def later_blocks(i, carry, plane=plane):
            items = []
            for j in range(unroll):
                start = (i * unroll + j + 1) * blk
                start = start if isinstance(start, int) else pl.multiple_of(start, blk)
                items.append((plane, pl.ds(start, blk), pl.ds(start - blk, 2 * blk), bias_band_ref))
            run(items)
            return carry

        if n_band == unroll:
            later_blocks(0, 0)
        else:
            lax.fori_loop(0, n_band // unroll, later_blocks, 0)


def _dil_attn(q, k, v):
    batch, dil, length, gw = q.shape
    planes = min(dil, max(1, ATT_MIN_BLOCKS * ATT_BLOCK // length))
    spec = lambda width: pl.BlockSpec((None, planes, length, width), lambda b, r: (b, r, 0, 0))
    blk = ATT_BLOCK
    vmem = (2 * 4 * planes * length * gw * 2 + 4 * planes * length * LANES * 4 + 3 * 2 * blk * 2 * blk * 4
            + 48 * blk * 2 * blk * 4 + (8 << 20))
    stat = jax.ShapeDtypeStruct((batch, dil, length, LANES), F32)
    return pl.pallas_call(
        _dil_attn_kernel,
        grid=(batch, dil // planes),
        in_specs=[spec(gw), spec(gw), spec(gw)],
        out_specs=[spec(gw), spec(LANES), spec(LANES)],
        out_shape=[jax.ShapeDtypeStruct((batch, dil, length, gw), BF16), stat, stat],
        scratch_shapes=[pltpu.VMEM((2 * blk, blk), F32), pltpu.VMEM((2 * blk, 2 * blk), F32)],
        compiler_params=_params(2, vmem),
        name=f"dil_attn_d{dil}",
    )(q, k, v)


def kernel(x, ret_w_in, ret_w_out, ret_gn_gain, kv_norm, att_w_kv, att_w_q, att_w_o, norm_mix_pre, norm_mix_post, norm_ffn_pre, norm_ffn_post, ffn_w_up, ffn_conv_w, ffn_conv_b, ffn_w_down):
    batch, seq, d = x.shape
    depth = norm_mix_pre.shape[0]
    n_ret = ret_w_in.shape[0]
    assert seq % TOKEN_TILE == 0 and seq % RET_CHUNK == 0
    assert all(seq % (dil * ATT_BLOCK) == 0 and TOKEN_TILE % (16 * dil) == 0 for dil in DILATIONS)
    assert all(win // dil == ATT_BLOCK for win, dil in DIL_GROUPS)

    rcos, rsin, acos, aneg, apos = _rope_tables(seq)
    att_tables = (acos, apos, aneg)
    log_gamma = jnp.log(1.0 - 2.0 ** (-5.0 - jnp.arange(RET_HEADS, dtype=F32)))
    gain = lambda g: g[None, :]
    bf = lambda w: w.astype(BF16)

    xt = x.reshape(batch * seq, d)
    keys = values = None
    for layer in range(depth):
        if layer < n_ret:
            q, k, v, g = _ret_inproj(xt, gain(norm_mix_pre[layer]), bf(ret_w_in[layer]), rcos, rsin, seq)
            y = _retention(q, k, v, g, gain(ret_gn_gain[layer]), log_gamma, batch, seq)
            xt = _outproj(xt, y, bf(ret_w_out[layer]), gain(norm_mix_post[layer]))
        else:
            bi = layer - n_ret
            queries = _att_proj(xt, gain(norm_mix_pre[layer]), bf(att_w_q[bi]), att_tables, N_GROUPS, batch, seq,
                                scale=ATT_HEAD_DIM ** -0.5 * math.log2(math.e))
            outs, maxes, dens = zip(*[_dil_attn(queries[gi], keys[gi], values[gi]) for gi in range(N_GROUPS)])
            xt = _att_merge(xt, outs, maxes, dens, bf(att_w_o[bi]), gain(norm_mix_post[layer]), seq)
        xt = _ffn(xt, gain(norm_ffn_pre[layer]), bf(ffn_w_up[layer]), ffn_conv_w[layer], gain(ffn_conv_b[layer]),
                  bf(ffn_w_down[layer]), gain(norm_ffn_post[layer]), seq)
        if layer == n_ret - 1:
            kv = _att_proj(xt, gain(kv_norm), bf(att_w_kv), att_tables, N_GROUPS, batch, seq)
            keys, values = kv[:N_GROUPS], kv[N_GROUPS:]
    return xt.reshape(batch, seq, d)
```

```python
import functools
import math

import jax
import jax.numpy as jnp
from jax import lax
from jax.experimental import pallas as pl
from jax.experimental.pallas import tpu as pltpu

F32 = jnp.float32
BF16 = jnp.bfloat16

LANES = 128
V7X_VMEM_BYTES = 64 * 1024 * 1024

RMS_EPS = 1e-6
GN_EPS = 1e-6
RET_HEADS = 4
RET_QK_DIM = 256
RET_V_DIM = 512
RET_ROT_BASE = 10000.0
ATT_HEADS = 16
ATT_HEAD_DIM = 64
ATT_GROUP_WIDTH = ATT_HEADS * ATT_HEAD_DIM
DIL_GROUPS = ((128, 1), (512, 4), (2048, 16))
DILATIONS = tuple(d for _, d in DIL_GROUPS)
N_GROUPS = 3
ATT_BLOCK = 128
ROPE_THETA = 500000.0
ROPE_DIMS = ATT_HEAD_DIM // 4
CONV_WIDTH = 3

TOKEN_TILE = 512
WIDE_TILE = 1024
COL_CHUNK = 512
FFN_CHUNK = 2816
CONV_HALO = 16
RET_CHUNK = 256
RET_SKEW = 1
ATT_MIN_BLOCKS = 4
ATT_SKEW = 4
ATT_UNROLL = 5
NEG_BIG = -1e30


def _vmem_limit(nbytes):
    return int(min(V7X_VMEM_BYTES - (4 << 20), max(nbytes, 16 << 20)))


def _params(n_axes, vmem_bytes):
    return pltpu.CompilerParams(dimension_semantics=("arbitrary",) * n_axes,
                                vmem_limit_bytes=_vmem_limit(vmem_bytes))


def _resident(shape):
    return pl.BlockSpec(shape, lambda *_: (0,) * len(shape), pipeline_mode=pl.Buffered(1))


def _rms(x, gain):
    return x * lax.rsqrt(jnp.mean(x * x, axis=-1, keepdims=True) + RMS_EPS) * gain


def _dot(a, b):
    return jnp.dot(a, b, preferred_element_type=F32)


def _dot_nt(a, b):
    return lax.dot_general(a, b, (((1,), (1,)), ((), ())), preferred_element_type=F32)


def _rope_tables_kernel(rfreq_ref, afreq_ref, aneg_ref, apos_ref, rcos_ref, rsin_ref, *att_refs):
    seq = rcos_ref.shape[0]
    row = lax.broadcasted_iota(jnp.int32, (seq, LANES), 0)
    rang = row.astype(F32) * rfreq_ref[...]
    rcos_ref[...] = jnp.cos(rang)
    rsin_ref[...] = jnp.sin(rang)
    tm = TOKEN_TILE
    for i, dil in enumerate(DILATIONS):
        per = tm // dil
        in_tile = row & (tm - 1)
        pos = (row - in_tile) + (in_tile & (per - 1)) * dil + lax.shift_right_logical(in_tile, per.bit_length() - 1)
        ang = pos.astype(F32) * afreq_ref[...]
        sin = jnp.sin(ang)
        att_refs[3 * i][...] = jnp.cos(ang)
        att_refs[3 * i + 1][...] = sin * apos_ref[...]
        att_refs[3 * i + 2][...] = sin * aneg_ref[...]


def _rope_tables(seq):
    rfreq = 1.0 / (RET_ROT_BASE ** jnp.linspace(0.0, 1.0, RET_QK_DIM // 2, dtype=F32))
    inv = ROPE_THETA ** (-jnp.arange(0, ROPE_DIMS, 2, dtype=F32) / ROPE_DIMS)
    half = ROPE_DIMS // 2
    lane = jnp.arange(LANES) % ATT_HEAD_DIM
    afreq = jnp.where(lane < ROPE_DIMS, inv[lane % half], 0.0).astype(F32)
    aneg = jnp.where(lane < half, -1.0, 0.0).astype(F32)
    apos = jnp.where((lane >= half) & (lane < ROPE_DIMS), 1.0, 0.0).astype(F32)
    tab = jax.ShapeDtypeStruct((seq, LANES), F32)
    tabs = pl.pallas_call(
        _rope_tables_kernel,
        out_shape=(tab,) * (2 + 3 * N_GROUPS),
        name="rope_tables",
    )(rfreq[None, :], afreq[None, :], aneg[None, :], apos[None, :])
    return tabs[0], tabs[1], tabs[2:]


def _partial_rotary(t, cos, pos, neg):
    return t * cos + pltpu.roll(t, 8, 1) * pos + pltpu.roll(t, LANES - 8, 1) * neg


def _ret_inproj_kernel(lg_ref, x_ref, gain_ref, w_ref, cos_ref, sin_ref, gn_ref, q_ref, k_ref, v_ref, g_ref):
    tm = x_ref.shape[0]
    hb = _rms(x_ref[...], gain_ref[...]).astype(BF16)
    cos = cos_ref[...]
    sin = sin_ref[...]
    dqk = RET_HEADS * RET_QK_DIM
    dv = RET_HEADS * RET_V_DIM
    half = RET_QK_DIM // 2
    n1 = (lax.broadcasted_iota(jnp.int32, (tm, 1), 0) % RET_CHUNK + 1).astype(F32)
    n_chunks = w_ref.shape[1] // COL_CHUNK
    project = lambda c: _dot(hb, w_ref[:, c * COL_CHUNK:(c + 1) * COL_CHUNK])
    ahead = project(0)
    for c in range(n_chunks):
        col = c * COL_CHUNK
        acc, ahead = ahead, (project(c + 1) if c + 1 < n_chunks else None)
        if col < 2 * dqk:
            is_q = col < dqk
            dst, base = (q_ref, col) if is_q else (k_ref, col - dqk)
            for hh in range(COL_CHUNK // RET_QK_DIM):
                o = base + hh * RET_QK_DIM
                lg = lg_ref[o // RET_QK_DIM]
                scale = jnp.exp(n1 * lg) if is_q else jnp.exp(-n1 * lg) * RET_QK_DIM ** -0.5
                a = acc[:, hh * RET_QK_DIM:hh * RET_QK_DIM + half]
                b = acc[:, hh * RET_QK_DIM + half:(hh + 1) * RET_QK_DIM]
                lo = (a * cos - b * sin) * scale
                hi = (b * cos + a * sin) * scale
                if is_q:
                    dst[:, o:o + half] = lo.astype(BF16)
                    dst[:, o + half:o + RET_QK_DIM] = hi.astype(BF16)
                else:
                    dst[o:o + half, :] = lo.T.astype(BF16)
                    dst[o + half:o + RET_QK_DIM, :] = hi.T.astype(BF16)
        elif col < 2 * dqk + dv:
            o = col - 2 * dqk
            v_ref[:, o:o + COL_CHUNK] = acc.astype(BF16)
        else:
            o = col - 2 * dqk - dv
            g_ref[:, o:o + COL_CHUNK] = (acc / (1.0 + jnp.exp(-acc)) * gn_ref[:, o:o + COL_CHUNK]).astype(BF16)


def _ret_inproj(x, gain, w, cos, sin, gn_gain, log_gamma, seq):
    tokens, d = x.shape
    tm = WIDE_TILE
    tps = seq // tm
    assert tm % RET_CHUNK == 0
    dqk = RET_HEADS * RET_QK_DIM
    dv = RET_HEADS * RET_V_DIM
    row = lambda i: (i, 0)
    tab = pl.BlockSpec((tm, LANES), lambda i: (i % tps, 0))
    vmem = 2 * tm * d * 4 + w.size * 2 + 2 * tm * (2 * dqk + 2 * dv) * 2 + 8 * tm * COL_CHUNK * 4 + (8 << 20)
    return pl.pallas_call(
        _ret_inproj_kernel,
        grid=(tokens // tm,),
        in_specs=[pl.BlockSpec(memory_space=pltpu.SMEM), pl.BlockSpec((tm, d), row), _resident((1, d)),
                  _resident(w.shape), tab, tab, _resident((1, dv))],
        out_specs=[pl.BlockSpec((tm, dqk), row), pl.BlockSpec((dqk, tm), lambda i: (0, i)),
                   pl.BlockSpec((tm, dv), row), pl.BlockSpec((tm, dv), row)],
        out_shape=[jax.ShapeDtypeStruct((tokens, dqk), BF16), jax.ShapeDtypeStruct((dqk, tokens), BF16),
                   jax.ShapeDtypeStruct((tokens, dv), BF16), jax.ShapeDtypeStruct((tokens, dv), BF16)],
        compiler_params=_params(1, vmem),
        name="ret_inproj",
    )(log_gamma, x, gain, w, cos, sin, gn_gain)


def _retention_kernel(lg_ref, q_ref, kt_ref, v_ref, g_ref, o_ref, state_ref):
    head = pl.program_id(1)
    lg = lg_ref[head]
    seq = q_ref.shape[0]
    ck = RET_CHUNK
    n_chunks = seq // ck
    causal = lax.broadcasted_iota(jnp.int32, (ck, ck), 0) >= lax.broadcasted_iota(jnp.int32, (ck, ck), 1)
    chunk_decay = jnp.exp(jnp.full((1, RET_V_DIM), ck, F32) * lg)

    def state_free(c):
        rows = pl.ds(c * ck, ck)
        qc = q_ref[rows, :]
        kt = kt_ref[:, rows]
        vc = v_ref[rows, :]
        scores = jnp.where(causal, _dot(qc, kt), 0.0)
        intra = _dot(scores.astype(BF16), vc)
        upd = _dot(kt, vc) if c + 1 < n_chunks else None
        return qc, intra, upd

    def finish(c, qc, intra, upd):
        rows = pl.ds(c * ck, ck)
        y = intra + _dot(qc, state_ref[...].astype(BF16))
        if upd is not None:
            state_ref[...] = (state_ref[...] + upd) * chunk_decay
        mu = jnp.mean(y, axis=-1, keepdims=True)
        yc = y - mu
        var = jnp.mean(yc * yc, axis=-1, keepdims=True)
        yn = (yc * lax.rsqrt(var + GN_EPS)).astype(BF16)
        o_ref[rows, :] = yn * g_ref[rows, :]

    state_ref[...] = jnp.zeros_like(state_ref)
    ahead = [state_free(c) for c in range(min(RET_SKEW, n_chunks))]
    for c in range(n_chunks):
        if c + RET_SKEW < n_chunks:
            ahead.append(state_free(c + RET_SKEW))
        finish(c, *ahead.pop(0))


def _retention(q, kt, v, g, log_gamma, batch, seq):
    tokens = q.shape[0]
    dv = RET_HEADS * RET_V_DIM
    q_spec = pl.BlockSpec((seq, RET_QK_DIM), lambda b, h: (b, h))
    kt_spec = pl.BlockSpec((RET_QK_DIM, seq), lambda b, h: (h, b))
    v_spec = pl.BlockSpec((seq, RET_V_DIM), lambda b, h: (b, h))
    vmem = 2 * seq * (2 * RET_QK_DIM + 3 * RET_V_DIM) * 2 + 16 * RET_CHUNK * RET_V_DIM * 4 + (8 << 20)
    return pl.pallas_call(
        _retention_kernel,
        grid=(batch, RET_HEADS),
        in_specs=[pl.BlockSpec(memory_space=pltpu.SMEM), q_spec, kt_spec, v_spec, v_spec],
        out_specs=v_spec,
        out_shape=jax.ShapeDtypeStruct((tokens, dv), BF16),
        scratch_shapes=[pltpu.VMEM((RET_QK_DIM, RET_V_DIM), F32)],
        compiler_params=_params(2, vmem),
        name="retention",
    )(log_gamma, q, kt, v, g)


def _outproj_kernel(x_ref, a_ref, w_ref, gain_ref, o_ref):
    m = _dot(a_ref[...], w_ref[...])
    o_ref[...] = x_ref[...] + _rms(m, gain_ref[...])


def _outproj(x, a, w, gain):
    tokens, d = x.shape
    kdim = a.shape[1]
    tm = WIDE_TILE
    row = lambda i: (i, 0)
    vmem = 4 * tm * d * 4 + 2 * tm * kdim * 2 + w.size * 2 + 4 * tm * d * 4 + (8 << 20)
    return pl.pallas_call(
        _outproj_kernel,
        grid=(tokens // tm,),
        in_specs=[pl.BlockSpec((tm, d), row), pl.BlockSpec((tm, kdim), row), _resident(w.shape), _resident((1, d))],
        out_specs=pl.BlockSpec((tm, d), row),
        out_shape=jax.ShapeDtypeStruct((tokens, d), F32),
        compiler_params=_params(1, vmem),
        name="outproj",
    )(x, a, w, gain)


def _natural_rows(src_ref, lane0, stage_ref, slab, dilation):
    if dilation == 1:
        return src_ref[0, :, lane0:lane0 + LANES].astype(F32)
    per = src_ref.shape[1]
    for r in range(dilation):
        stage_ref[slab, pl.ds(r, per, stride=dilation), :] = src_ref[r, :, lane0:lane0 + LANES].astype(F32)
    return stage_ref[slab]


def _att_merge_kernel(x_ref, o0_ref, o1_ref, o2_ref, m0_ref, m1_ref, m2_ref, d0_ref, d1_ref, d2_ref,
                      expand_ref, w_ref, gain_ref, out_ref, stage_ref, merged_ref):
    o_refs = (o0_ref, o1_ref, o2_ref)
    ms = [_natural_rows(r, 0, stage_ref, g, DILATIONS[g]) for g, r in enumerate((m0_ref, m1_ref, m2_ref))]
    dens = [_natural_rows(r, 0, stage_ref, N_GROUPS + g, DILATIONS[g]) for g, r in enumerate((d0_ref, d1_ref, d2_ref))]
    top = jnp.maximum(jnp.maximum(ms[0], ms[1]), ms[2])
    es = [jnp.exp2(m - top) for m in ms]
    inv = 1.0 / (es[0] * dens[0] + es[1] * dens[1] + es[2] * dens[2])
    lane = lax.broadcasted_iota(jnp.int32, top.shape, 1)
    packed = None
    for g, e in enumerate(es):
        alpha = jnp.where(lane < ATT_HEADS, e * inv, 0.0)
        hi = alpha.astype(BF16).astype(F32)
        lo = alpha - hi
        for part, piece in enumerate((hi, lo)):
            shift = (2 * g + part) * ATT_HEADS
            piece = pltpu.roll(piece, shift, 1) if shift else piece
            packed = piece if packed is None else packed + piece
    wide = _dot(packed.astype(BF16), expand_ref[...])
    for j in range(ATT_GROUP_WIDTH // LANES):
        acc = None
        for g in range(N_GROUPS):
            slab = 2 * N_GROUPS + g * (ATT_GROUP_WIDTH // LANES) + j
            o = _natural_rows(o_refs[g], j * LANES, stage_ref, slab, DILATIONS[g])
            c0 = g * ATT_GROUP_WIDTH + j * LANES
            term = wide[:, c0:c0 + LANES] * o
            acc = term if acc is None else acc + term
        merged_ref[:, j * LANES:(j + 1) * LANES] = acc.astype(BF16)
    m = _dot(merged_ref[...], w_ref[...])
    out_ref[...] = x_ref[...] + _rms(m, gain_ref[...])


def _att_merge(x, outs, maxes, dens, w, gain, seq):
    tokens, d = x.shape
    tm = WIDE_TILE
    tps = seq // tm
    row = lambda i: (i, 0)
    gw = ATT_GROUP_WIDTH

    def plane(width, dil):
        return pl.BlockSpec((None, dil, tm // dil, width), lambda i: (i // tps, 0, i % tps, 0))

    col = jnp.arange(N_GROUPS * gw)
    src = 2 * ATT_HEADS * (col // gw) + (col % gw) // ATT_HEAD_DIM
    lane = jnp.arange(LANES)[:, None]
    expand = ((lane == src[None, :]) | (lane == src[None, :] + ATT_HEADS)).astype(BF16)
    n_stage = 2 * N_GROUPS + N_GROUPS * (gw // LANES)
    vmem = (4 * tm * d * 4 + 6 * tm * gw * 2 + 12 * tm * LANES * 4 + w.size * 2 + expand.size * 2
            + n_stage * tm * LANES * 4 + 8 * tm * d * 4 + (8 << 20))
    return pl.pallas_call(
        _att_merge_kernel,
        grid=(tokens // tm,),
        in_specs=[pl.BlockSpec((tm, d), row)] + [plane(gw, dil) for dil in DILATIONS]
                 + [plane(LANES, dil) for dil in DILATIONS] * 2
                 + [_resident(expand.shape), _resident(w.shape), _resident((1, d))],
        out_specs=pl.BlockSpec((tm, d), row),
        out_shape=jax.ShapeDtypeStruct((tokens, d), F32),
        scratch_shapes=[pltpu.VMEM((n_stage, tm, LANES), F32), pltpu.VMEM((tm, gw), BF16)],
        compiler_params=_params(1, vmem),
        name="att_merge",
    )(x, *outs, *maxes, *dens, expand, w, gain)


def _ffn_kernel(x_ref, halo_ref, gpre_ref, wup_ref, cw_ref, cb_ref, wdown_ref, gpost_ref, o_ref, *, tiles_per_seq):
    tm = x_ref.shape[0]
    d_ff = wdown_ref.shape[0]
    x = x_ref[...]
    gpre = gpre_ref[...]
    h = _rms(x, gpre)
    hh = _rms(halo_ref[...], gpre)
    hb_ext = jnp.concatenate([hh, h], axis=0).astype(BF16)
    hb = hb_ext[CONV_HALO:]
    seq_start = (pl.program_id(0) % tiles_per_seq) == 0
    ext_row = lax.broadcasted_iota(jnp.int32, (CONV_HALO + tm, 1), 0)
    keep = jnp.logical_or(ext_row >= CONV_HALO, jnp.logical_not(seq_start))
    acc = jnp.zeros((tm, o_ref.shape[1]), F32)
    for col in range(0, d_ff, FFN_CHUNK):
        width = min(FFN_CHUNK, d_ff - col)
        gate = _dot(hb_ext, wup_ref[:, col:col + width])
        gate = jnp.where(keep, gate, 0.0)
        val = _dot(hb, wup_ref[:, d_ff + col:d_ff + col + width])
        cw = cw_ref[:, col:col + width]
        conv = cb_ref[:, col:col + width] + cw[2:3] * gate[CONV_HALO:]
        conv = conv + cw[1:2] * gate[CONV_HALO - 1:CONV_HALO - 1 + tm]
        conv = conv + cw[0:1] * gate[CONV_HALO - 2:CONV_HALO - 2 + tm]
        gelu = 0.5 * conv * (1.0 + jnp.tanh(math.sqrt(2.0 / math.pi) * (conv + 0.044715 * (conv * conv * conv))))
        acc = acc + _dot((gelu * val).astype(BF16), wdown_ref[col:col + width, :])
    o_ref[...] = x + _rms(acc, gpost_ref[...])


def _ffn(x, gpre, wup, cw, cb, wdown, gpost, seq):
    tokens, d = x.shape
    d_ff = wdown.shape[0]
    tm = WIDE_TILE
    tps = seq // tm
    halo_blocks = tm // CONV_HALO
    row = lambda i: (i, 0)
    halo_spec = pl.BlockSpec((CONV_HALO, d), lambda i: (jnp.maximum(i * halo_blocks - 1, 0), 0))
    vmem = 4 * tm * d * 4 + (wup.size + wdown.size) * 2 + 6 * tm * d * 4 + 16 * tm * FFN_CHUNK * 4 + (8 << 20)
    return pl.pallas_call(
        functools.partial(_ffn_kernel, tiles_per_seq=tps),
        grid=(tokens // tm,),
        in_specs=[pl.BlockSpec((tm, d), row), halo_spec, _resident((1, d)), _resident(wup.shape),
                  _resident(cw.shape), _resident((1, d_ff)), _resident(wdown.shape), _resident((1, d))],
        out_specs=pl.BlockSpec((tm, d), row),
        out_shape=jax.ShapeDtypeStruct((tokens, d), F32),
        compiler_params=_params(1, vmem),
        name="conv_ffn",
    )(x, x, gpre, wup, cw, cb, wdown, gpost)


def _att_proj_kernel(x_ref, gain_ref, w_ref, *refs, dilations, n_rope):
    n_tab = 3 * N_GROUPS
    tables, outs = refs[:n_tab], refs[n_tab:n_tab + len(dilations)]
    stage_ref, perm_ref = refs[n_tab + len(dilations):]
    tm, d = x_ref.shape
    h = _rms(x_ref[...], gain_ref[...])
    lhs = {1: h.astype(BF16)}
    for j in range(d // LANES):
        stage_ref[j] = h[:, j * LANES:(j + 1) * LANES]
    for pi, dil in enumerate(sorted(set(dilations) - {1})):
        per = tm // dil
        for j in range(d // LANES):
            for r in range(dil):
                piece = stage_ref[j, pl.ds(r, per, stride=dil), :]
                perm_ref[pi, r * per:(r + 1) * per, j * LANES:(j + 1) * LANES] = piece.astype(BF16)
        lhs[dil] = perm_ref.at[pi]
    work = [(s, c0) for s in sorted(range(len(outs)), key=lambda i: dilations[i])
            for c0 in range(0, ATT_GROUP_WIDTH, COL_CHUNK)]

    def project(s, c0):
        dil = dilations[s]
        col = s * ATT_GROUP_WIDTH + c0
        return _dot(lhs[dil] if dil == 1 else lhs[dil][...], w_ref[:, col:col + COL_CHUNK])

    ahead = project(*work[0])
    for i, (s, c0) in enumerate(work):
        acc, ahead = ahead, (project(*work[i + 1]) if i + 1 < len(work) else None)
        o_ref, dil = outs[s], dilations[s]
        gi = DILATIONS.index(dil)
        cos, pos, neg = (tables[3 * gi + k][...] for k in range(3))
        per = tm // dil
        for j in range(COL_CHUNK // LANES):
            t = acc[:, j * LANES:(j + 1) * LANES]
            if s < n_rope:
                t = _partial_rotary(t, cos, pos, neg)
            t = t.astype(BF16)
            lane0 = c0 + j * LANES
            for r in range(dil):
                o_ref[r, :, lane0:lane0 + LANES] = t[r * per:(r + 1) * per]


def _att_proj(x, gain, w, tables, n_rope, batch, seq):
    tokens, d = x.shape
    n = w.shape[1]
    gw = ATT_GROUP_WIDTH
    dilations = DILATIONS * (n // (N_GROUPS * gw))
    n_perm = len(set(dilations) - {1})
    tm = TOKEN_TILE
    tps = seq // tm
    row = lambda i: (i, 0)
    tab = pl.BlockSpec((tm, LANES), lambda i: (i % tps, 0))
    plane = lambda dil: pl.BlockSpec((None, dil, tm // dil, gw), lambda i: (i // tps, 0, i % tps, 0))
    vmem = (2 * tm * d * 4 + w.size * 2 + 2 * tm * n * 2 + 8 * tm * COL_CHUNK * 4 + 2 * len(tables) * tm * LANES * 4
            + tm * d * 4 + n_perm * tm * d * 2 + (8 << 20))
    return pl.pallas_call(
        functools.partial(_att_proj_kernel, dilations=dilations, n_rope=n_rope),
        grid=(tokens // tm,),
        in_specs=[pl.BlockSpec((tm, d), row), _resident((1, d)), _resident(w.shape)] + [tab] * len(tables),
        out_specs=[plane(dil) for dil in dilations],
        out_shape=[jax.ShapeDtypeStruct((batch, dil, seq // dil, gw), BF16) for dil in dilations],
        scratch_shapes=[pltpu.VMEM((d // LANES, tm, LANES), F32), pltpu.VMEM((n_perm, tm, d), BF16)],
        compiler_params=_params(1, vmem),
        name="att_proj",
    )(x, gain, w, *tables)


def _dil_attn_kernel(q_ref, k_ref, v_ref, o_ref, m_ref, d_ref, bias_first_ref, bias_band_ref):
    n_planes, length = q_ref.shape[0], q_ref.shape[1]
    blk = ATT_BLOCK
    n_blocks = length // blk
    n_pairs = q_ref.shape[2] // LANES
    lane = lax.broadcasted_iota(jnp.int32, (blk, LANES), 1)
    low = lane < ATT_HEAD_DIM
    zero = jnp.zeros((blk, LANES), BF16)

    def bias(n_keys):
        qi = lax.broadcasted_iota(jnp.int32, (2 * blk, n_keys), 0) % blk
        kj = lax.broadcasted_iota(jnp.int32, (2 * blk, n_keys), 1)
        dist = qi + (n_keys - blk) - kj
        return jnp.where(jnp.logical_and(dist >= 0, dist <= blk), 0.0, NEG_BIG).astype(F32)

    bias_first_ref[...] = bias(blk)
    bias_band_ref[...] = bias(2 * blk)

    def scores(plane, q_rows, k_rows, bias_ref, hp):
        cols = slice(hp * LANES, (hp + 1) * LANES)
        q2 = q_ref[plane, q_rows, cols]
        qq = jnp.concatenate([jnp.where(low, q2, zero), jnp.where(low, zero, q2)], axis=0)
        return _dot_nt(qq, k_ref[plane, k_rows, cols]) + bias_ref[...]

    def softmax(s):
        m = jnp.max(s, axis=-1, keepdims=True)
        return m, jnp.exp2(s - m).astype(BF16)

    def finish(plane, q_rows, k_rows, bias_ref, hp, mp):
        m, p = mp
        cols = slice(hp * LANES, (hp + 1) * LANES)
        ones = jnp.ones((bias_ref.shape[1], LANES), BF16)
        pv = _dot(p, jnp.concatenate([v_ref[plane, k_rows, cols], ones], axis=1))
        o_ref[plane, q_rows, cols] = jnp.where(low, pv[:blk, :LANES], pv[blk:, :LANES]).astype(BF16)
        for half in range(2):
            head = 2 * hp + half
            rows = slice(half * blk, (half + 1) * blk)
            m_ref[plane, q_rows, head:head + 1] = m[rows]
            d_ref[plane, q_rows, head:head + 1] = pv[rows, LANES + head:LANES + head + 1]

    def run(items):
        work = [(*item, hp) for item in items for hp in range(n_pairs)]
        for plane, q_rows, _, _ in items:
            m_ref[plane, q_rows, :] = jnp.zeros((blk, LANES), F32)
            d_ref[plane, q_rows, :] = jnp.ones((blk, LANES), F32)
        n = len(work)
        s_vals, mp_vals = {}, {}
        for t in range(-ATT_SKEW, n):
            if t + ATT_SKEW < n:
                s_vals[t + ATT_SKEW] = scores(*work[t + ATT_SKEW])
            mid = t + ATT_SKEW // 2
            if 0 <= mid < n and ATT_SKEW > 1:
                mp_vals[mid] = softmax(s_vals.pop(mid))
            if t >= 0:
                if ATT_SKEW <= 1:
                    mp_vals[t] = softmax(s_vals.pop(t))
                finish(*work[t], mp_vals.pop(t))

    run([(plane, pl.ds(0, blk), pl.ds(0, blk), bias_first_ref) for plane in range(n_planes)])
    n_band = n_blocks - 1
    unroll = max(u for u in range(1, ATT_UNROLL + 1) if n_band % u == 0) if n_band else 0
    for plane in range(n_planes if n_band else 0):
        def later_blocks(i, carry, plane=plane):
            items = []
            for j in range(unroll):
                start = (i * unroll + j + 1) * blk
                start = start if isinstance(start, int) else pl.multiple_of(start, blk)
                items.append((plane, pl.ds(start, blk), pl.ds(start - blk, 2 * blk), bias_band_ref))
            run(items)
            return carry

        if n_band == unroll:
            later_blocks(0, 0)
        else:
            lax.fori_loop(0, n_band // unroll, later_blocks, 0)


def _dil_attn(q, k, v):
    batch, dil, length, gw = q.shape
    planes = min(dil, max(1, ATT_MIN_BLOCKS * ATT_BLOCK // length))
    spec = lambda width: pl.BlockSpec((None, planes, length, width), lambda b, r: (b, r, 0, 0))
    blk = ATT_BLOCK
    vmem = (2 * 4 * planes * length * gw * 2 + 4 * planes * length * LANES * 4 + 3 * 2 * blk * 2 * blk * 4
            + 48 * blk * 2 * blk * 4 + (8 << 20))
    stat = jax.ShapeDtypeStruct((batch, dil, length, LANES), F32)
    return pl.pallas_call(
        _dil_attn_kernel,
        grid=(batch, dil // planes),
        in_specs=[spec(gw), spec(gw), spec(gw)],
        out_specs=[spec(gw), spec(LANES), spec(LANES)],
        out_shape=[jax.ShapeDtypeStruct((batch, dil, length, gw), BF16), stat, stat],
        scratch_shapes=[pltpu.VMEM((2 * blk, blk), F32), pltpu.VMEM((2 * blk, 2 * blk), F32)],
        compiler_params=_params(2, vmem),
        name=f"dil_attn_d{dil}",
    )(q, k, v)


def kernel(x, ret_w_in, ret_w_out, ret_gn_gain, kv_norm, att_w_kv, att_w_q, att_w_o, norm_mix_pre, norm_mix_post, norm_ffn_pre, norm_ffn_post, ffn_w_up, ffn_conv_w, ffn_conv_b, ffn_w_down):
    batch, seq, d = x.shape
    depth = norm_mix_pre.shape[0]
    n_ret = ret_w_in.shape[0]
    assert seq % TOKEN_TILE == 0 and seq % WIDE_TILE == 0 and seq % RET_CHUNK == 0
    assert all(seq % (dil * ATT_BLOCK) == 0 and TOKEN_TILE % (16 * dil) == 0 for dil in DILATIONS)
    assert all(win // dil == ATT_BLOCK for win, dil in DIL_GROUPS)

    rcos, rsin, att_tables = _rope_tables(seq)
    q_scale = ATT_HEAD_DIM ** -0.5 * math.log2(math.e)
    log_gamma = jnp.log(1.0 - 2.0 ** (-5.0 - jnp.arange(RET_HEADS, dtype=F32)))
    gain = lambda g: g[None, :]
    bf = lambda w: w.astype(BF16)

    xt = x.reshape(batch * seq, d)
    keys = values = None
    for layer in range(depth):
        if layer < n_ret:
            q, k, v, g = _ret_inproj(xt, gain(norm_mix_pre[layer]), bf(ret_w_in[layer]), rcos, rsin,
                                     gain(ret_gn_gain[layer]), log_gamma, seq)
            y = _retention(q, k, v, g, log_gamma, batch, seq)
            xt = _outproj(xt, y, bf(ret_w_out[layer]), gain(norm_mix_post[layer]))
        else:
            bi = layer - n_ret
            queries = _att_proj(xt, gain(norm_mix_pre[layer]), bf(att_w_q[bi] * q_scale), att_tables, N_GROUPS,
                                batch, seq)
            outs, maxes, dens = zip(*[_dil_attn(queries[gi], keys[gi], values[gi]) for gi in range(N_GROUPS)])
            xt = _att_merge(xt, outs, maxes, dens, bf(att_w_o[bi]), gain(norm_mix_post[layer]), seq)
        xt = _ffn(xt, gain(norm_ffn_pre[layer]), bf(ffn_w_up[layer]), ffn_conv_w[layer], gain(ffn_conv_b[layer]),
                  bf(ffn_w_down[layer]), gain(norm_ffn_post[layer]), seq)
        if layer == n_ret - 1:
            kv = _att_proj(xt, gain(kv_norm), bf(att_w_kv), att_tables, N_GROUPS, batch, seq)
            keys, values = kv[:N_GROUPS], kv[N_GROUPS:]
    return xt.reshape(batch, seq, d)
```

```python
import functools
import math

import jax
import jax.numpy as jnp
import numpy as np
from jax import lax
from jax.experimental import pallas as pl
from jax.experimental.pallas import tpu as pltpu

F32 = jnp.float32
BF16 = jnp.bfloat16

LANES = 128
V7X_VMEM_BYTES = 64 * 1024 * 1024

RMS_EPS = 1e-6
GN_EPS = 1e-6
RET_HEADS = 4
RET_QK_DIM = 256
RET_V_DIM = 512
RET_ROT_BASE = 10000.0
ATT_HEADS = 16
ATT_HEAD_DIM = 64
ATT_GROUP_WIDTH = ATT_HEADS * ATT_HEAD_DIM
DIL_GROUPS = ((128, 1), (512, 4), (2048, 16))
DILATIONS = tuple(d for _, d in DIL_GROUPS)
N_GROUPS = 3
ATT_BLOCK = 128
ROPE_THETA = 500000.0
ROPE_DIMS = ATT_HEAD_DIM // 4
CONV_WIDTH = 3

TOKEN_TILE = 512
WIDE_TILE = 1024
COL_CHUNK = 512
FFN_SUB = 512
MERGE_SUB = 512
CONV_HALO = 16
RET_CHUNK = 256
RET_SKEW = 1
ATT_MIN_BLOCKS = 16
ATT_SKEW = 4
ATT_UNROLL = 15
NEG_BIG = -1e30


def _vmem_limit(nbytes):
    return int(min(V7X_VMEM_BYTES - (4 << 20), max(nbytes, 16 << 20)))


def _params(n_axes, vmem_bytes):
    return pltpu.CompilerParams(dimension_semantics=("arbitrary",) * n_axes,
                                vmem_limit_bytes=_vmem_limit(vmem_bytes))


def _resident(shape):
    return pl.BlockSpec(shape, lambda *_: (0,) * len(shape), pipeline_mode=pl.Buffered(1))


def _rms(x, gain):
    return x * lax.rsqrt(jnp.mean(x * x, axis=-1, keepdims=True) + RMS_EPS) * gain


def _dot(a, b):
    return jnp.dot(a, b, preferred_element_type=F32)


def _dot_nt(a, b):
    return lax.dot_general(a, b, (((1,), (1,)), ((), ())), preferred_element_type=F32)


def _qk_lane_layout():
    half = ROPE_DIMS // 2
    lane = np.arange(LANES)
    upper = lane >= ATT_HEAD_DIM
    local = lane - ATT_HEAD_DIM * upper
    is_b = np.where(upper, local >= half, (local >= half) & (local < ROPE_DIMS))
    dim = np.where(local < ROPE_DIMS, local % half + half * upper, local)
    dim = np.where(upper & (local >= ROPE_DIMS), local, dim)
    return is_b.astype(np.int32), dim.astype(np.int32)


def _qk_slab_columns(w):
    is_b, dim = _qk_lane_layout()
    heads = w.reshape(w.shape[0], -1, 2, ATT_HEAD_DIM)
    starts = [0] + [i for i in range(1, LANES) if is_b[i] != is_b[i - 1] or dim[i] != dim[i - 1] + 1] + [LANES]
    runs = [heads[:, :, int(is_b[lo]), int(dim[lo]):int(dim[lo]) + hi - lo] for lo, hi in zip(starts[:-1], starts[1:])]
    return jnp.concatenate(runs, axis=-1).reshape(w.shape)


def _rope_tables_kernel(rfreq_ref, afreq_ref, asign_ref, rcos_ref, rsin_ref, *att_refs):
    seq = rcos_ref.shape[0]
    row = lax.broadcasted_iota(jnp.int32, (seq, LANES), 0)
    rang = row.astype(F32) * rfreq_ref[...]
    rcos_ref[...] = jnp.cos(rang)
    rsin_ref[...] = jnp.sin(rang)
    tm = TOKEN_TILE
    for i, dil in enumerate(DILATIONS):
        per = tm // dil
        in_tile = row & (tm - 1)
        pos = (row - in_tile) + (in_tile & (per - 1)) * dil + lax.shift_right_logical(in_tile, per.bit_length() - 1)
        ang = pos.astype(F32) * afreq_ref[...]
        att_refs[2 * i][...] = jnp.cos(ang)
        att_refs[2 * i + 1][...] = jnp.sin(ang) * asign_ref[...]


def _rope_tables(seq):
    rfreq = 1.0 / (RET_ROT_BASE ** jnp.linspace(0.0, 1.0, RET_QK_DIM // 2, dtype=F32))
    inv = ROPE_THETA ** (-jnp.arange(0, ROPE_DIMS, 2, dtype=F32) / ROPE_DIMS)
    half = ROPE_DIMS // 2
    _, dim = _qk_lane_layout()
    afreq = jnp.where(dim < ROPE_DIMS, inv[dim % half], 0.0).astype(F32)
    asign = jnp.asarray(np.where(dim < half, -1.0, np.where(dim < ROPE_DIMS, 1.0, 0.0)), F32)
    tab = jax.ShapeDtypeStruct((seq, LANES), F32)
    tabs = pl.pallas_call(
        _rope_tables_kernel,
        out_shape=(tab,) * (2 + 2 * N_GROUPS),
        name="rope_tables",
    )(rfreq[None, :], afreq[None, :], asign[None, :])
    return tabs[0], tabs[1], tabs[2:]


def _partial_rotary(t, cos, signed_sin):
    return t * cos + pltpu.roll(t, ATT_HEAD_DIM, 1) * signed_sin


def _ret_inproj_kernel(lg_ref, x_ref, gain_ref, w_ref, cos_ref, sin_ref, gn_ref, q_ref, k_ref, v_ref, g_ref):
    tm = x_ref.shape[0]
    hb = _rms(x_ref[...], gain_ref[...]).astype(BF16)
    cos = cos_ref[...]
    sin = sin_ref[...]
    dqk = RET_HEADS * RET_QK_DIM
    dv = RET_HEADS * RET_V_DIM
    half = RET_QK_DIM // 2
    n1 = (lax.broadcasted_iota(jnp.int32, (tm, 1), 0) % RET_CHUNK + 1).astype(F32)
    n_chunks = w_ref.shape[1] // COL_CHUNK
    project = lambda c: _dot(hb, w_ref[:, c * COL_CHUNK:(c + 1) * COL_CHUNK])
    ahead = project(0)
    for c in range(n_chunks):
        col = c * COL_CHUNK
        acc, ahead = ahead, (project(c + 1) if c + 1 < n_chunks else None)
        if col < 2 * dqk:
            is_q = col < dqk
            dst, base = (q_ref, col) if is_q else (k_ref, col - dqk)
            for hh in range(COL_CHUNK // RET_QK_DIM):
                o = base + hh * RET_QK_DIM
                lg = lg_ref[o // RET_QK_DIM]
                scale = jnp.exp(n1 * lg) if is_q else jnp.exp(-n1 * lg) * RET_QK_DIM ** -0.5
                a = acc[:, hh * RET_QK_DIM:hh * RET_QK_DIM + half]
                b = acc[:, hh * RET_QK_DIM + half:(hh + 1) * RET_QK_DIM]
                lo = (a * cos - b * sin) * scale
                hi = (b * cos + a * sin) * scale
                if is_q:
                    dst[:, o:o + half] = lo.astype(BF16)
                    dst[:, o + half:o + RET_QK_DIM] = hi.astype(BF16)
                else:
                    dst[o:o + half, :] = lo.T.astype(BF16)
                    dst[o + half:o + RET_QK_DIM, :] = hi.T.astype(BF16)
        elif col < 2 * dqk + dv:
            o = col - 2 * dqk
            v_ref[:, o:o + COL_CHUNK] = acc.astype(BF16)
        else:
            o = col - 2 * dqk - dv
            g_ref[:, o:o + COL_CHUNK] = (acc / (1.0 + jnp.exp(-acc)) * gn_ref[:, o:o + COL_CHUNK]).astype(BF16)


def _ret_inproj(x, gain, w, cos, sin, gn_gain, log_gamma, seq):
    tokens, d = x.shape
    tm = WIDE_TILE
    tps = seq // tm
    assert tm % RET_CHUNK == 0
    dqk = RET_HEADS * RET_QK_DIM
    dv = RET_HEADS * RET_V_DIM
    row = lambda i: (i, 0)
    tab = pl.BlockSpec((tm, LANES), lambda i: (i % tps, 0))
    vmem = 2 * tm * d * 4 + w.size * 2 + 2 * tm * (2 * dqk + 2 * dv) * 2 + 8 * tm * COL_CHUNK * 4 + (8 << 20)
    return pl.pallas_call(
        _ret_inproj_kernel,
        grid=(tokens // tm,),
        in_specs=[pl.BlockSpec(memory_space=pltpu.SMEM), pl.BlockSpec((tm, d), row), _resident((1, d)),
                  _resident(w.shape), tab, tab, _resident((1, dv))],
        out_specs=[pl.BlockSpec((tm, dqk), row), pl.BlockSpec((dqk, tm), lambda i: (0, i)),
                   pl.BlockSpec((tm, dv), row), pl.BlockSpec((tm, dv), row)],
        out_shape=[jax.ShapeDtypeStruct((tokens, dqk), BF16), jax.ShapeDtypeStruct((dqk, tokens), BF16),
                   jax.ShapeDtypeStruct((tokens, dv), BF16), jax.ShapeDtypeStruct((tokens, dv), BF16)],
        compiler_params=_params(1, vmem),
        name="ret_inproj",
    )(log_gamma, x, gain, w, cos, sin, gn_gain)


def _retention_kernel(lg_ref, q_ref, kt_ref, v_ref, g_ref, o_ref, state_ref):
    head = pl.program_id(1)
    lg = lg_ref[head]
    seq = q_ref.shape[0]
    ck = RET_CHUNK
    n_chunks = seq // ck
    causal = lax.broadcasted_iota(jnp.int32, (ck, ck), 0) >= lax.broadcasted_iota(jnp.int32, (ck, ck), 1)
    chunk_decay = jnp.exp(jnp.full((1, RET_V_DIM), ck, F32) * lg)

    def state_free(c):
        rows = pl.ds(c * ck, ck)
        qc = q_ref[rows, :]
        kt = kt_ref[:, rows]
        vc = v_ref[rows, :]
        scores = jnp.where(causal, _dot(qc, kt), 0.0)
        intra = _dot(scores.astype(BF16), vc)
        upd = _dot(kt, vc) if c + 1 < n_chunks else None
        return qc, intra, upd

    def finish(c, qc, intra, upd):
        rows = pl.ds(c * ck, ck)
        y = intra + _dot(qc, state_ref[...].astype(BF16))
        if upd is not None:
            state_ref[...] = (state_ref[...] + upd) * chunk_decay
        mu = jnp.mean(y, axis=-1, keepdims=True)
        yc = y - mu
        var = jnp.mean(yc * yc, axis=-1, keepdims=True)
        yn = (yc * lax.rsqrt(var + GN_EPS)).astype(BF16)
        o_ref[rows, :] = yn * g_ref[rows, :]

    state_ref[...] = jnp.zeros_like(state_ref)
    ahead = [state_free(c) for c in range(min(RET_SKEW, n_chunks))]
    for c in range(n_chunks):
        if c + RET_SKEW < n_chunks:
            ahead.append(state_free(c + RET_SKEW))
        finish(c, *ahead.pop(0))


def _retention(q, kt, v, g, log_gamma, batch, seq):
    tokens = q.shape[0]
    dv = RET_HEADS * RET_V_DIM
    q_spec = pl.BlockSpec((seq, RET_QK_DIM), lambda b, h: (b, h))
    kt_spec = pl.BlockSpec((RET_QK_DIM, seq), lambda b, h: (h, b))
    v_spec = pl.BlockSpec((seq, RET_V_DIM), lambda b, h: (b, h))
    vmem = 2 * seq * (2 * RET_QK_DIM + 3 * RET_V_DIM) * 2 + 16 * RET_CHUNK * RET_V_DIM * 4 + (8 << 20)
    return pl.pallas_call(
        _retention_kernel,
        grid=(batch, RET_HEADS),
        in_specs=[pl.BlockSpec(memory_space=pltpu.SMEM), q_spec, kt_spec, v_spec, v_spec],
        out_specs=v_spec,
        out_shape=jax.ShapeDtypeStruct((tokens, dv), BF16),
        scratch_shapes=[pltpu.VMEM((RET_QK_DIM, RET_V_DIM), F32)],
        compiler_params=_params(2, vmem),
        name="retention",
    )(log_gamma, q, kt, v, g)


def _outproj_kernel(x_ref, a_ref, w_ref, gain_ref, o_ref):
    m = _dot(a_ref[...], w_ref[...])
    o_ref[...] = x_ref[...] + _rms(m, gain_ref[...])


def _outproj(x, a, w, gain):
    tokens, d = x.shape
    kdim = a.shape[1]
    tm = WIDE_TILE
    row = lambda i: (i, 0)
    vmem = 4 * tm * d * 4 + 2 * tm * kdim * 2 + w.size * 2 + 4 * tm * d * 4 + (8 << 20)
    return pl.pallas_call(
        _outproj_kernel,
        grid=(tokens // tm,),
        in_specs=[pl.BlockSpec((tm, d), row), pl.BlockSpec((tm, kdim), row), _resident(w.shape), _resident((1, d))],
        out_specs=pl.BlockSpec((tm, d), row),
        out_shape=jax.ShapeDtypeStruct((tokens, d), F32),
        compiler_params=_params(1, vmem),
        name="outproj",
    )(x, a, w, gain)


def _natural_rows(src_ref, row0, rows, lane0, stage_ref, slab, dilation):
    per, first = rows // dilation, row0 // dilation
    if dilation == 1:
        return src_ref[0, first:first + per, lane0:lane0 + LANES].astype(F32)
    for r in range(dilation):
        piece = src_ref[r, first:first + per, lane0:lane0 + LANES]
        stage_ref[slab, pl.ds(r, per, stride=dilation), :] = piece.astype(F32)
    return stage_ref[slab]


def _att_merge_kernel(x_ref, o0_ref, o1_ref, o2_ref, m0_ref, m1_ref, m2_ref, d0_ref, d1_ref, d2_ref,
                      expand_ref, w_ref, gain_ref, out_ref, stage_ref, merged_ref):
    o_refs = (o0_ref, o1_ref, o2_ref)
    tm = x_ref.shape[0]
    sub = min(MERGE_SUB, tm)
    n_sub = tm // sub
    n_slabs = ATT_GROUP_WIDTH // LANES
    per_sub = stage_ref.shape[0] // n_sub

    def weights(s):
        nat = lambda ref, k, g: _natural_rows(ref, s * sub, sub, 0, stage_ref, s * per_sub + k, DILATIONS[g])
        ms = [nat(r, g, g) for g, r in enumerate((m0_ref, m1_ref, m2_ref))]
        dens = [nat(r, N_GROUPS + g, g) for g, r in enumerate((d0_ref, d1_ref, d2_ref))]
        top = jnp.maximum(jnp.maximum(ms[0], ms[1]), ms[2])
        es = [jnp.exp2(m - top) for m in ms]
        inv = 1.0 / (es[0] * dens[0] + es[1] * dens[1] + es[2] * dens[2])
        lane = lax.broadcasted_iota(jnp.int32, top.shape, 1)
        packed = None
        for g, e in enumerate(es):
            alpha = jnp.where(lane < ATT_HEADS, e * inv, 0.0)
            hi = alpha.astype(BF16).astype(F32)
            lo = alpha - hi
            for part, piece in enumerate((hi, lo)):
                shift = (2 * g + part) * ATT_HEADS
                piece = pltpu.roll(piece, shift, 1) if shift else piece
                packed = piece if packed is None else packed + piece
        return _dot(packed.astype(BF16), expand_ref[...])

    def finish(s, wide):
        rows = slice(s * sub, (s + 1) * sub)
        for j in range(n_slabs):
            acc = None
            for g in range(N_GROUPS):
                slab = s * per_sub + 2 * N_GROUPS + g * n_slabs + j
                o = _natural_rows(o_refs[g], s * sub, sub, j * LANES, stage_ref, slab, DILATIONS[g])
                c0 = g * ATT_GROUP_WIDTH + j * LANES
                term = wide[:, c0:c0 + LANES] * o
                acc = term if acc is None else acc + term
            merged_ref[rows, j * LANES:(j + 1) * LANES] = acc.astype(BF16)
        m = _dot(merged_ref[rows, :], w_ref[...])
        out_ref[rows, :] = x_ref[rows, :] + _rms(m, gain_ref[...])

    ahead = weights(0)
    for s in range(n_sub):
        current, ahead = ahead, (weights(s + 1) if s + 1 < n_sub else None)
        finish(s, current)


def _att_merge(x, outs, maxes, dens, w, gain, seq):
    tokens, d = x.shape
    tm = WIDE_TILE
    sub = min(MERGE_SUB, tm)
    tps = seq // tm
    row = lambda i: (i, 0)
    gw = ATT_GROUP_WIDTH

    def plane(width, dil):
        return pl.BlockSpec((None, dil, tm // dil, width), lambda i: (i // tps, 0, i % tps, 0))

    col = jnp.arange(N_GROUPS * gw)
    src = 2 * ATT_HEADS * (col // gw) + (col % gw) // ATT_HEAD_DIM
    lane = jnp.arange(LANES)[:, None]
    expand = ((lane == src[None, :]) | (lane == src[None, :] + ATT_HEADS)).astype(BF16)
    n_stage = 2 * N_GROUPS + N_GROUPS * (gw // LANES)
    vmem = (4 * tm * d * 4 + 6 * tm * gw * 2 + 12 * tm * LANES * 4 + w.size * 2 + expand.size * 2
            + n_stage * tm * LANES * 4 + 8 * tm * d * 4 + (8 << 20))
    return pl.pallas_call(
        _att_merge_kernel,
        grid=(tokens // tm,),
        in_specs=[pl.BlockSpec((tm, d), row)] + [plane(gw, dil) for dil in DILATIONS]
                 + [plane(LANES, dil) for dil in DILATIONS] * 2
                 + [_resident(expand.shape), _resident(w.shape), _resident((1, d))],
        out_specs=pl.BlockSpec((tm, d), row),
        out_shape=jax.ShapeDtypeStruct((tokens, d), F32),
        scratch_shapes=[pltpu.VMEM((n_stage * (tm // sub), sub, LANES), F32), pltpu.VMEM((tm, gw), BF16)],
        compiler_params=_params(1, vmem),
        name="att_merge",
    )(x, *outs, *maxes, *dens, expand, w, gain)


def _ffn_kernel(x_ref, halo_ref, gpre_ref, wup_ref, cw_ref, cb_ref, wdown_ref, gpost_ref, o_ref, *, tiles_per_seq):
    tm = x_ref.shape[0]
    sub = min(FFN_SUB, tm)
    n_sub = tm // sub
    d_ff = wdown_ref.shape[0]
    gpre = gpre_ref[...]
    seq_start = (pl.program_id(0) % tiles_per_seq) == 0
    ext_row = lax.broadcasted_iota(jnp.int32, (CONV_HALO + sub, 1), 0)
    keep = jnp.logical_or(ext_row >= CONV_HALO, jnp.logical_not(seq_start))
    cw, cb = cw_ref[...], cb_ref[...]

    def up(s):
        h = _rms(x_ref[s * sub:(s + 1) * sub, :], gpre)
        if s == 0:
            before = _rms(halo_ref[...], gpre)
        else:
            before = _rms(x_ref[s * sub - CONV_HALO:s * sub, :], gpre)
        hb_ext = jnp.concatenate([before, h], axis=0).astype(BF16)
        gate = _dot(hb_ext, wup_ref[:, :d_ff])
        if s == 0:
            gate = jnp.where(keep, gate, 0.0)
        return gate, _dot(hb_ext[CONV_HALO:], wup_ref[:, d_ff:])

    def mix(gate, val):
        conv = cb + cw[2:3] * gate[CONV_HALO:]
        conv = conv + cw[1:2] * gate[CONV_HALO - 1:CONV_HALO - 1 + sub]
        conv = conv + cw[0:1] * gate[CONV_HALO - 2:CONV_HALO - 2 + sub]
        gelu = 0.5 * conv * (1.0 + jnp.tanh(math.sqrt(2.0 / math.pi) * (conv + 0.044715 * (conv * conv * conv))))
        return _dot((gelu * val).astype(BF16), wdown_ref[...])

    def finish(s, acc):
        rows = slice(s * sub, (s + 1) * sub)
        o_ref[rows, :] = x_ref[rows, :] + _rms(acc, gpost_ref[...])

    ahead, behind = up(0), None
    for s in range(n_sub):
        current, ahead = ahead, (up(s + 1) if s + 1 < n_sub else None)
        acc = mix(*current)
        if behind is not None:
            finish(*behind)
        behind = (s, acc)
    finish(*behind)


def _ffn(x, gpre, wup, cw, cb, wdown, gpost, seq):
    tokens, d = x.shape
    d_ff = wdown.shape[0]
    tm = WIDE_TILE
    tps = seq // tm
    halo_blocks = tm // CONV_HALO
    row = lambda i: (i, 0)
    halo_spec = pl.BlockSpec((CONV_HALO, d), lambda i: (jnp.maximum(i * halo_blocks - 1, 0), 0))
    vmem = 4 * tm * d * 4 + (wup.size + wdown.size) * 2 + 6 * tm * d * 4 + 16 * tm * d_ff * 4 + (8 << 20)
    return pl.pallas_call(
        functools.partial(_ffn_kernel, tiles_per_seq=tps),
        grid=(tokens // tm,),
        in_specs=[pl.BlockSpec((tm, d), row), halo_spec, _resident((1, d)), _resident(wup.shape),
                  _resident(cw.shape), _resident((1, d_ff)), _resident(wdown.shape), _resident((1, d))],
        out_specs=pl.BlockSpec((tm, d), row),
        out_shape=jax.ShapeDtypeStruct((tokens, d), F32),
        compiler_params=_params(1, vmem),
        name="conv_ffn",
    )(x, x, gpre, wup, cw, cb, wdown, gpost)


def _att_proj_kernel(x_ref, gain_ref, w_ref, *refs, dilations, n_rope):
    n_tab = 2 * N_GROUPS
    tables, outs = refs[:n_tab], refs[n_tab:n_tab + len(dilations)]
    stage_ref, perm_ref = refs[n_tab + len(dilations):]
    tm, d = x_ref.shape
    h = _rms(x_ref[...], gain_ref[...])
    lhs = {1: h.astype(BF16)}
    for j in range(d // LANES):
        stage_ref[j] = h[:, j * LANES:(j + 1) * LANES]
    for pi, dil in enumerate(sorted(set(dilations) - {1})):
        per = tm // dil
        for j in range(d // LANES):
            for r in range(dil):
                piece = stage_ref[j, pl.ds(r, per, stride=dil), :]
                perm_ref[pi, r * per:(r + 1) * per, j * LANES:(j + 1) * LANES] = piece.astype(BF16)
        lhs[dil] = perm_ref.at[pi]
    work = [(s, c0) for s in sorted(range(len(outs)), key=lambda i: dilations[i])
            for c0 in range(0, ATT_GROUP_WIDTH, COL_CHUNK)]

    def project(s, c0):
        dil = dilations[s]
        col = s * ATT_GROUP_WIDTH + c0
        return _dot(lhs[dil] if dil == 1 else lhs[dil][...], w_ref[:, col:col + COL_CHUNK])

    ahead = project(*work[0])
    for i, (s, c0) in enumerate(work):
        acc, ahead = ahead, (project(*work[i + 1]) if i + 1 < len(work) else None)
        o_ref, dil = outs[s], dilations[s]
        gi = DILATIONS.index(dil)
        cos, signed_sin = tables[2 * gi][...], tables[2 * gi + 1][...]
        per = tm // dil
        for j in range(COL_CHUNK // LANES):
            t = acc[:, j * LANES:(j + 1) * LANES]
            if s < n_rope:
                t = _partial_rotary(t, cos, signed_sin)
            t = t.astype(BF16)
            lane0 = c0 + j * LANES
            for r in range(dil):
                o_ref[r, :, lane0:lane0 + LANES] = t[r * per:(r + 1) * per]


def _att_proj(x, gain, w, tables, n_rope, batch, seq):
    tokens, d = x.shape
    n = w.shape[1]
    gw = ATT_GROUP_WIDTH
    dilations = DILATIONS * (n // (N_GROUPS * gw))
    n_perm = len(set(dilations) - {1})
    tm = TOKEN_TILE
    tps = seq // tm
    row = lambda i: (i, 0)
    tab = pl.BlockSpec((tm, LANES), lambda i: (i % tps, 0))
    plane = lambda dil: pl.BlockSpec((None, dil, tm // dil, gw), lambda i: (i // tps, 0, i % tps, 0))
    vmem = (2 * tm * d * 4 + w.size * 2 + 2 * tm * n * 2 + 8 * tm * COL_CHUNK * 4 + 2 * len(tables) * tm * LANES * 4
            + tm * d * 4 + n_perm * tm * d * 2 + (8 << 20))
    return pl.pallas_call(
        functools.partial(_att_proj_kernel, dilations=dilations, n_rope=n_rope),
        grid=(tokens // tm,),
        in_specs=[pl.BlockSpec((tm, d), row), _resident((1, d)), _resident(w.shape)] + [tab] * len(tables),
        out_specs=[plane(dil) for dil in dilations],
        out_shape=[jax.ShapeDtypeStruct((batch, dil, seq // dil, gw), BF16) for dil in dilations],
        scratch_shapes=[pltpu.VMEM((d // LANES, tm, LANES), F32), pltpu.VMEM((n_perm, tm, d), BF16)],
        compiler_params=_params(1, vmem),
        name="att_proj",
    )(x, gain, w, *tables)


def _dil_attn_kernel(q_ref, k_ref, v_ref, o_ref, m_ref, d_ref, bias_first_ref, bias_band_ref):
    n_planes, length = q_ref.shape[0], q_ref.shape[1]
    blk = ATT_BLOCK
    n_blocks = length // blk
    n_pairs = q_ref.shape[2] // LANES
    lane = lax.broadcasted_iota(jnp.int32, (blk, LANES), 1)
    low = lane < ATT_HEAD_DIM
    is_b, _ = _qk_lane_layout()
    edges = [0] + [i for i in range(1, LANES) if is_b[i] != is_b[i - 1]] + [LANES]
    q_first = None
    for lo, hi in zip(edges[:-1], edges[1:]):
        if not is_b[lo]:
            run = jnp.logical_and(lane >= lo, lane < hi)
            q_first = run if q_first is None else jnp.logical_or(q_first, run)
    zero = jnp.zeros((blk, LANES), BF16)

    def bias(n_keys):
        qi = lax.broadcasted_iota(jnp.int32, (2 * blk, n_keys), 0) % blk
        kj = lax.broadcasted_iota(jnp.int32, (2 * blk, n_keys), 1)
        dist = qi + (n_keys - blk) - kj
        return jnp.where(jnp.logical_and(dist >= 0, dist <= blk), 0.0, NEG_BIG).astype(F32)

    bias_first_ref[...] = bias(blk)
    bias_band_ref[...] = bias(2 * blk)

    def scores(plane, q_rows, k_rows, bias_ref, hp):
        cols = slice(hp * LANES, (hp + 1) * LANES)
        q2 = q_ref[plane, q_rows, cols]
        qq = jnp.concatenate([jnp.where(q_first, q2, zero), jnp.where(q_first, zero, q2)], axis=0)
        return _dot_nt(qq, k_ref[plane, k_rows, cols]) + bias_ref[...]

    def softmax(s):
        m = jnp.max(s, axis=-1, keepdims=True)
        return m, jnp.exp2(s - m).astype(BF16)

    def finish(plane, q_rows, k_rows, bias_ref, hp, mp):
        m, p = mp
        cols = slice(hp * LANES, (hp + 1) * LANES)
        ones = jnp.ones((bias_ref.shape[1], LANES), BF16)
        pv = _dot(p, jnp.concatenate([v_ref[plane, k_rows, cols], ones], axis=1))
        o_ref[plane, q_rows, cols] = jnp.where(low, pv[:blk, :LANES], pv[blk:, :LANES]).astype(BF16)
        for half in range(2):
            head = 2 * hp + half
            rows = slice(half * blk, (half + 1) * blk)
            m_ref[plane, q_rows, head:head + 1] = m[rows]
            d_ref[plane, q_rows, head:head + 1] = pv[rows, LANES + head:LANES + head + 1]

    def run(items):
        work = [(*item, hp) for item in items for hp in range(n_pairs)]
        for plane, q_rows, _, _ in items:
            m_ref[plane, q_rows, :] = jnp.zeros((blk, LANES), F32)
            d_ref[plane, q_rows, :] = jnp.ones((blk, LANES), F32)
        n = len(work)
        s_vals, mp_vals = {}, {}
        for t in range(-ATT_SKEW, n):
            if t + ATT_SKEW < n:
                s_vals[t + ATT_SKEW] = scores(*work[t + ATT_SKEW])
            mid = t + ATT_SKEW // 2
            if 0 <= mid < n and ATT_SKEW > 1:
                mp_vals[mid] = softmax(s_vals.pop(mid))
            if t >= 0:
                if ATT_SKEW <= 1:
                    mp_vals[t] = softmax(s_vals.pop(t))
                finish(*work[t], mp_vals.pop(t))

    run([(plane, pl.ds(0, blk), pl.ds(0, blk), bias_first_ref) for plane in range(n_planes)])
    n_band = n_blocks - 1
    unroll = max(u for u in range(1, ATT_UNROLL + 1) if n_band % u == 0) if n_band else 0
    for plane in range(n_planes if n_band else 0):
        def later_blocks(i, carry, plane=plane):
            items = []
            for j in range(unroll):
                start = (i * unroll + j + 1) * blk
                start = start if isinstance(start, int) else pl.multiple_of(start, blk)
                items.append((plane, pl.ds(start, blk), pl.ds(start - blk, 2 * blk), bias_band_ref))
            run(items)
            return carry

        if n_band == unroll:
            later_blocks(0, 0)
        else:
            lax.fori_loop(0, n_band // unroll, later_blocks, 0)


def _dil_attn(q, k, v):
    batch, dil, length, gw = q.shape
    planes = min(dil, max(1, ATT_MIN_BLOCKS * ATT_BLOCK // length))
    spec = lambda width: pl.BlockSpec((None, planes, length, width), lambda b, r: (b, r, 0, 0))
    blk = ATT_BLOCK
    vmem = (2 * 4 * planes * length * gw * 2 + 4 * planes * length * LANES * 4 + 3 * 2 * blk * 2 * blk * 4
            + 48 * blk * 2 * blk * 4 + (8 << 20))
    stat = jax.ShapeDtypeStruct((batch, dil, length, LANES), F32)
    return pl.pallas_call(
        _dil_attn_kernel,
        grid=(batch, dil // planes),
        in_specs=[spec(gw), spec(gw), spec(gw)],
        out_specs=[spec(gw), spec(LANES), spec(LANES)],
        out_shape=[jax.ShapeDtypeStruct((batch, dil, length, gw), BF16), stat, stat],
        scratch_shapes=[pltpu.VMEM((2 * blk, blk), F32), pltpu.VMEM((2 * blk, 2 * blk), F32)],
        compiler_params=_params(2, vmem),
        name=f"dil_attn_d{dil}",
    )(q, k, v)


def kernel(x, ret_w_in, ret_w_out, ret_gn_gain, kv_norm, att_w_kv, att_w_q, att_w_o, norm_mix_pre, norm_mix_post, norm_ffn_pre, norm_ffn_post, ffn_w_up, ffn_conv_w, ffn_conv_b, ffn_w_down):
    batch, seq, d = x.shape
    depth = norm_mix_pre.shape[0]
    n_ret = ret_w_in.shape[0]
    assert seq % TOKEN_TILE == 0 and seq % WIDE_TILE == 0 and seq % RET_CHUNK == 0
    assert all(seq % (dil * ATT_BLOCK) == 0 and TOKEN_TILE % (16 * dil) == 0 for dil in DILATIONS)
    assert all(win // dil == ATT_BLOCK for win, dil in DIL_GROUPS)

    rcos, rsin, att_tables = _rope_tables(seq)
    q_scale = ATT_HEAD_DIM ** -0.5 * math.log2(math.e)
    n_qk = N_GROUPS * ATT_GROUP_WIDTH
    log_gamma = jnp.log(1.0 - 2.0 ** (-5.0 - jnp.arange(RET_HEADS, dtype=F32)))
    gain = lambda g: g[None, :]
    bf = lambda w: w.astype(BF16)

    xt = x.reshape(batch * seq, d)
    keys = values = None
    for layer in range(depth):
        if layer < n_ret:
            q, k, v, g = _ret_inproj(xt, gain(norm_mix_pre[layer]), bf(ret_w_in[layer]), rcos, rsin,
                                     gain(ret_gn_gain[layer]), log_gamma, seq)
            y = _retention(q, k, v, g, log_gamma, batch, seq)
            xt = _outproj(xt, y, bf(ret_w_out[layer]), gain(norm_mix_post[layer]))
        else:
            bi = layer - n_ret
            queries = _att_proj(xt, gain(norm_mix_pre[layer]), bf(_qk_slab_columns(att_w_q[bi]) * q_scale),
                                att_tables, N_GROUPS, batch, seq)
            outs, maxes, dens = zip(*[_dil_attn(queries[gi], keys[gi], values[gi]) for gi in range(N_GROUPS)])
            xt = _att_merge(xt, outs, maxes, dens, bf(att_w_o[bi]), gain(norm_mix_post[layer]), seq)
        xt = _ffn(xt, gain(norm_ffn_pre[layer]), bf(ffn_w_up[layer]), ffn_conv_w[layer], gain(ffn_conv_b[layer]),
                  bf(ffn_w_down[layer]), gain(norm_ffn_post[layer]), seq)
        if layer == n_ret - 1:
            w_kv = jnp.concatenate([_qk_slab_columns(att_w_kv[:, :n_qk]), att_w_kv[:, n_qk:]], axis=1)
            kv = _att_proj(xt, gain(kv_norm), bf(w_kv), att_tables, N_GROUPS, batch, seq)
            keys, values = kv[:N_GROUPS], kv[N_GROUPS:]
    return xt.reshape(batch, seq, d)
```

```python
import functools
import math

import jax
import jax.numpy as jnp
import numpy as np
from jax import lax
from jax.experimental import pallas as pl
from jax.experimental.pallas import tpu as pltpu

F32 = jnp.float32
BF16 = jnp.bfloat16

LANES = 128
V7X_VMEM_BYTES = 64 * 1024 * 1024

RMS_EPS = 1e-6
GN_EPS = 1e-6
RET_HEADS = 4
RET_QK_DIM = 256
RET_V_DIM = 512
RET_ROT_BASE = 10000.0
ATT_HEADS = 16
ATT_HEAD_DIM = 64
ATT_GROUP_WIDTH = ATT_HEADS * ATT_HEAD_DIM
DIL_GROUPS = ((128, 1), (512, 4), (2048, 16))
DILATIONS = tuple(d for _, d in DIL_GROUPS)
N_GROUPS = 3
ATT_BLOCK = 128
ROPE_THETA = 500000.0
ROPE_DIMS = ATT_HEAD_DIM // 4
CONV_WIDTH = 3

TOKEN_TILE = 512
WIDE_TILE = 1024
COL_CHUNK = 512
FFN_SUB = 512
MERGE_SUB = 512
CONV_HALO = 16
RET_CHUNK = 256
RET_SKEW = 0
RET_PROJ_ROWS = 256
ATT_MIN_BLOCKS = 16
ATT_SKEW = 4
ATT_UNROLL = 15
NEG_BIG = -1e30


def _vmem_limit(nbytes):
    return int(min(V7X_VMEM_BYTES - (4 << 20), max(nbytes, 16 << 20)))


def _params(n_axes, vmem_bytes):
    return pltpu.CompilerParams(dimension_semantics=("arbitrary",) * n_axes,
                                vmem_limit_bytes=_vmem_limit(vmem_bytes))


def _resident(shape):
    return pl.BlockSpec(shape, lambda *_: (0,) * len(shape), pipeline_mode=pl.Buffered(1))


def _rms(x, gain):
    return x * lax.rsqrt(jnp.mean(x * x, axis=-1, keepdims=True) + RMS_EPS) * gain


def _dot(a, b):
    return jnp.dot(a, b, preferred_element_type=F32)


def _dot_nt(a, b):
    return lax.dot_general(a, b, (((1,), (1,)), ((), ())), preferred_element_type=F32)


def _qk_lane_layout():
    half = ROPE_DIMS // 2
    lane = np.arange(LANES)
    upper = lane >= ATT_HEAD_DIM
    local = lane - ATT_HEAD_DIM * upper
    is_b = np.where(upper, local >= half, (local >= half) & (local < ROPE_DIMS))
    dim = np.where(local < ROPE_DIMS, local % half + half * upper, local)
    dim = np.where(upper & (local >= ROPE_DIMS), local, dim)
    return is_b.astype(np.int32), dim.astype(np.int32)


def _qk_slab_columns(w):
    is_b, dim = _qk_lane_layout()
    heads = w.reshape(w.shape[0], -1, 2, ATT_HEAD_DIM)
    starts = [0] + [i for i in range(1, LANES) if is_b[i] != is_b[i - 1] or dim[i] != dim[i - 1] + 1] + [LANES]
    runs = [heads[:, :, int(is_b[lo]), int(dim[lo]):int(dim[lo]) + hi - lo] for lo, hi in zip(starts[:-1], starts[1:])]
    return jnp.concatenate(runs, axis=-1).reshape(w.shape)


def _rope_tables_kernel(rfreq_ref, afreq_ref, asign_ref, rcos_ref, rsin_ref, *att_refs):
    seq = rcos_ref.shape[0]
    row = lax.broadcasted_iota(jnp.int32, (seq, LANES), 0)
    rang = row.astype(F32) * rfreq_ref[...]
    rcos_ref[...] = jnp.cos(rang)
    rsin_ref[...] = jnp.sin(rang)
    tm = TOKEN_TILE
    for i, dil in enumerate(DILATIONS):
        per = tm // dil
        in_tile = row & (tm - 1)
        pos = (row - in_tile) + (in_tile & (per - 1)) * dil + lax.shift_right_logical(in_tile, per.bit_length() - 1)
        ang = pos.astype(F32) * afreq_ref[...]
        att_refs[2 * i][...] = jnp.cos(ang)
        att_refs[2 * i + 1][...] = jnp.sin(ang) * asign_ref[...]


def _rope_tables(seq):
    rfreq = 1.0 / (RET_ROT_BASE ** jnp.linspace(0.0, 1.0, RET_QK_DIM // 2, dtype=F32))
    inv = ROPE_THETA ** (-jnp.arange(0, ROPE_DIMS, 2, dtype=F32) / ROPE_DIMS)
    half = ROPE_DIMS // 2
    _, dim = _qk_lane_layout()
    afreq = jnp.where(dim < ROPE_DIMS, inv[dim % half], 0.0).astype(F32)
    asign = jnp.asarray(np.where(dim < half, -1.0, np.where(dim < ROPE_DIMS, 1.0, 0.0)), F32)
    tab = jax.ShapeDtypeStruct((seq, LANES), F32)
    tabs = pl.pallas_call(
        _rope_tables_kernel,
        out_shape=(tab,) * (2 + 2 * N_GROUPS),
        name="rope_tables",
    )(rfreq[None, :], afreq[None, :], asign[None, :])
    return tabs[0], tabs[1], tabs[2:]


def _partial_rotary(t, cos, signed_sin):
    return t * cos + pltpu.roll(t, ATT_HEAD_DIM, 1) * signed_sin


def _ret_mixer_kernel(lg_ref, x_ref, gain_ref, wq_ref, wk_ref, wv_ref, wg_ref, cos_ref, sin_ref, gn_ref, y_ref,
                      hb_ref, state_ref):
    head = pl.program_id(1)
    lg = lg_ref[head]
    seq = x_ref.shape[0]
    pt, ck = RET_PROJ_ROWS, RET_CHUNK
    half = RET_QK_DIM // 2

    @pl.when(head == 0)
    def _():
        for t in range(seq // pt):
            rows = slice(t * pt, (t + 1) * pt)
            hb_ref[rows, :] = _rms(x_ref[rows, :], gain_ref[...]).astype(BF16)

    n1 = (lax.broadcasted_iota(jnp.int32, (pt, 1), 0) % ck + 1).astype(F32)
    q_scale = jnp.exp(n1 * lg)
    k_scale = jnp.exp(-n1 * lg) * RET_QK_DIM ** -0.5
    causal = lax.broadcasted_iota(jnp.int32, (ck, ck), 0) >= lax.broadcasted_iota(jnp.int32, (ck, ck), 1)
    chunk_decay = jnp.exp(jnp.full((1, RET_V_DIM), ck, F32) * lg)
    gn = gn_ref[...]

    def project(t):
        rows = slice(t * pt, (t + 1) * pt)
        hb = hb_ref[rows, :]
        cos, sin = cos_ref[rows, :], sin_ref[rows, :]

        def rotated(acc, scale):
            a, b = acc[:, :half], acc[:, half:]
            return (a * cos - b * sin) * scale, (b * cos + a * sin) * scale

        q_lo, q_hi = rotated(_dot(hb, wq_ref[...]), q_scale)
        k_lo, k_hi = rotated(_dot(hb, wk_ref[...]), k_scale)
        q = jnp.concatenate([q_lo, q_hi], axis=1).astype(BF16)
        kt = jnp.concatenate([k_lo.T, k_hi.T], axis=0).astype(BF16)
        v = _dot(hb, wv_ref[...]).astype(BF16)
        gate = _dot(hb, wg_ref[...])
        g = (gate / (1.0 + jnp.exp(-gate)) * gn).astype(BF16)
        return q, kt, v, g

    def state_free(c, q, kt, v, g, j):
        rows = slice(j * ck, (j + 1) * ck)
        qc, ktc, vc = q[rows], kt[:, rows], v[rows]
        scores = jnp.where(causal, _dot(qc, ktc), 0.0)
        intra = _dot(scores.astype(BF16), vc)
        upd = _dot(ktc, vc) if (c + 1) * ck < seq else None
        return c, qc, intra, upd, g[rows]

    def finish(c, qc, intra, upd, gc):
        y = intra + _dot(qc, state_ref[...].astype(BF16))
        if upd is not None:
            state_ref[...] = (state_ref[...] + upd) * chunk_decay
        mu = jnp.mean(y, axis=-1, keepdims=True)
        yc = y - mu
        var = jnp.mean(yc * yc, axis=-1, keepdims=True)
        yn = (yc * lax.rsqrt(var + GN_EPS)).astype(BF16)
        y_ref[c * ck:(c + 1) * ck, :] = yn * gc

    state_ref[...] = jnp.zeros_like(state_ref)
    n_tiles = seq // pt
    ahead, pending = project(0), []
    for t in range(n_tiles):
        current, ahead = ahead, (project(t + 1) if t + 1 < n_tiles else None)
        for j in range(pt // ck):
            pending.append(state_free(t * (pt // ck) + j, *current, j))
            if len(pending) > RET_SKEW:
                finish(*pending.pop(0))
    for item in pending:
        finish(*item)


def _ret_mixer(x, gain, w, cos, sin, gn_gain, log_gamma, batch, seq):
    tokens, d = x.shape
    dqk = RET_HEADS * RET_QK_DIM
    dv = RET_HEADS * RET_V_DIM
    assert seq % RET_PROJ_ROWS == 0 and RET_PROJ_ROWS % RET_CHUNK == 0
    qk_blocks = dqk // RET_QK_DIM
    v_blocks = dv // RET_V_DIM
    wq = pl.BlockSpec((d, RET_QK_DIM), lambda b, h: (0, h))
    wk = pl.BlockSpec((d, RET_QK_DIM), lambda b, h: (0, qk_blocks + h))
    wv = pl.BlockSpec((d, RET_V_DIM), lambda b, h: (0, 2 * dqk // RET_V_DIM + h))
    wg = pl.BlockSpec((d, RET_V_DIM), lambda b, h: (0, 2 * dqk // RET_V_DIM + v_blocks + h))
    head_cols = pl.BlockSpec((seq, RET_V_DIM), lambda b, h: (b, h))
    vmem = (2 * seq * d * 4 + 2 * d * (2 * RET_QK_DIM + 2 * RET_V_DIM) * 2 + seq * d * 2 + 2 * seq * RET_V_DIM * 2
            + 4 * seq * LANES * 4 + 16 * RET_PROJ_ROWS * RET_V_DIM * 4 + (8 << 20))
    return pl.pallas_call(
        _ret_mixer_kernel,
        grid=(batch, RET_HEADS),
        in_specs=[pl.BlockSpec(memory_space=pltpu.SMEM), pl.BlockSpec((seq, d), lambda b, h: (b, 0)),
                  _resident((1, d)), wq, wk, wv, wg, _resident((seq, LANES)), _resident((seq, LANES)),
                  pl.BlockSpec((1, RET_V_DIM), lambda b, h: (0, h))],
        out_specs=head_cols,
        out_shape=jax.ShapeDtypeStruct((tokens, dv), BF16),
        scratch_shapes=[pltpu.VMEM((seq, d), BF16), pltpu.VMEM((RET_QK_DIM, RET_V_DIM), F32)],
        compiler_params=_params(2, vmem),
        name="ret_mixer",
    )(log_gamma, x, gain, w, w, w, w, cos, sin, gn_gain)


def _outproj_kernel(x_ref, a_ref, w_ref, gain_ref, o_ref):
    m = _dot(a_ref[...], w_ref[...])
    o_ref[...] = x_ref[...] + _rms(m, gain_ref[...])


def _outproj(x, a, w, gain):
    tokens, d = x.shape
    kdim = a.shape[1]
    tm = WIDE_TILE
    row = lambda i: (i, 0)
    vmem = 4 * tm * d * 4 + 2 * tm * kdim * 2 + w.size * 2 + 4 * tm * d * 4 + (8 << 20)
    return pl.pallas_call(
        _outproj_kernel,
        grid=(tokens // tm,),
        in_specs=[pl.BlockSpec((tm, d), row), pl.BlockSpec((tm, kdim), row), _resident(w.shape), _resident((1, d))],
        out_specs=pl.BlockSpec((tm, d), row),
        out_shape=jax.ShapeDtypeStruct((tokens, d), F32),
        compiler_params=_params(1, vmem),
        name="outproj",
    )(x, a, w, gain)


def _natural_rows(src_ref, row0, rows, lane0, stage_ref, slab, dilation):
    per, first = rows // dilation, row0 // dilation
    if dilation == 1:
        return src_ref[0, first:first + per, lane0:lane0 + LANES].astype(F32)
    for r in range(dilation):
        piece = src_ref[r, first:first + per, lane0:lane0 + LANES]
        stage_ref[slab, pl.ds(r, per, stride=dilation), :] = piece.astype(F32)
    return stage_ref[slab]


def _att_merge_kernel(x_ref, o0_ref, o1_ref, o2_ref, m0_ref, m1_ref, m2_ref, d0_ref, d1_ref, d2_ref,
                      expand_ref, w_ref, gain_ref, out_ref, stage_ref, merged_ref):
    o_refs = (o0_ref, o1_ref, o2_ref)
    tm = x_ref.shape[0]
    sub = min(MERGE_SUB, tm)
    n_sub = tm // sub
    n_slabs = ATT_GROUP_WIDTH // LANES
    per_sub = stage_ref.shape[0] // n_sub

    def weights(s):
        nat = lambda ref, k, g: _natural_rows(ref, s * sub, sub, 0, stage_ref, s * per_sub + k, DILATIONS[g])
        ms = [nat(r, g, g) for g, r in enumerate((m0_ref, m1_ref, m2_ref))]
        dens = [nat(r, N_GROUPS + g, g) for g, r in enumerate((d0_ref, d1_ref, d2_ref))]
        top = jnp.maximum(jnp.maximum(ms[0], ms[1]), ms[2])
        es = [jnp.exp2(m - top) for m in ms]
        inv = 1.0 / (es[0] * dens[0] + es[1] * dens[1] + es[2] * dens[2])
        lane = lax.broadcasted_iota(jnp.int32, top.shape, 1)
        packed = None
        for g, e in enumerate(es):
            alpha = jnp.where(lane < ATT_HEADS, e * inv, 0.0)
            hi = alpha.astype(BF16).astype(F32)
            lo = alpha - hi
            for part, piece in enumerate((hi, lo)):
                shift = (2 * g + part) * ATT_HEADS
                piece = pltpu.roll(piece, shift, 1) if shift else piece
                packed = piece if packed is None else packed + piece
        return _dot(packed.astype(BF16), expand_ref[...])

    def finish(s, wide):
        rows = slice(s * sub, (s + 1) * sub)
        for j in range(n_slabs):
            acc = None
            for g in range(N_GROUPS):
                slab = s * per_sub + 2 * N_GROUPS + g * n_slabs + j
                o = _natural_rows(o_refs[g], s * sub, sub, j * LANES, stage_ref, slab, DILATIONS[g])
                c0 = g * ATT_GROUP_WIDTH + j * LANES
                term = wide[:, c0:c0 + LANES] * o
                acc = term if acc is None else acc + term
            merged_ref[rows, j * LANES:(j + 1) * LANES] = acc.astype(BF16)
        m = _dot(merged_ref[rows, :], w_ref[...])
        out_ref[rows, :] = x_ref[rows, :] + _rms(m, gain_ref[...])

    ahead = weights(0)
    for s in range(n_sub):
        current, ahead = ahead, (weights(s + 1) if s + 1 < n_sub else None)
        finish(s, current)


def _att_merge(x, outs, maxes, dens, w, gain, seq):
    tokens, d = x.shape
    tm = WIDE_TILE
    sub = min(MERGE_SUB, tm)
    tps = seq // tm
    row = lambda i: (i, 0)
    gw = ATT_GROUP_WIDTH

    def plane(width, dil):
        return pl.BlockSpec((None, dil, tm // dil, width), lambda i: (i // tps, 0, i % tps, 0))

    col = jnp.arange(N_GROUPS * gw)
    src = 2 * ATT_HEADS * (col // gw) + (col % gw) // ATT_HEAD_DIM
    lane = jnp.arange(LANES)[:, None]
    expand = ((lane == src[None, :]) | (lane == src[None, :] + ATT_HEADS)).astype(BF16)
    n_stage = 2 * N_GROUPS + N_GROUPS * (gw // LANES)
    vmem = (4 * tm * d * 4 + 6 * tm * gw * 2 + 12 * tm * LANES * 4 + w.size * 2 + expand.size * 2
            + n_stage * tm * LANES * 4 + 8 * tm * d * 4 + (8 << 20))
    return pl.pallas_call(
        _att_merge_kernel,
        grid=(tokens // tm,),
        in_specs=[pl.BlockSpec((tm, d), row)] + [plane(gw, dil) for dil in DILATIONS]
                 + [plane(LANES, dil) for dil in DILATIONS] * 2
                 + [_resident(expand.shape), _resident(w.shape), _resident((1, d))],
        out_specs=pl.BlockSpec((tm, d), row),
        out_shape=jax.ShapeDtypeStruct((tokens, d), F32),
        scratch_shapes=[pltpu.VMEM((n_stage * (tm // sub), sub, LANES), F32), pltpu.VMEM((tm, gw), BF16)],
        compiler_params=_params(1, vmem),
        name="att_merge",
    )(x, *outs, *maxes, *dens, expand, w, gain)


def _ffn_kernel(x_ref, halo_ref, gpre_ref, wup_ref, cw_ref, cb_ref, wdown_ref, gpost_ref, o_ref, *, tiles_per_seq):
    tm = x_ref.shape[0]
    sub = min(FFN_SUB, tm)
    n_sub = tm // sub
    d_ff = wdown_ref.shape[0]
    gpre = gpre_ref[...]
    seq_start = (pl.program_id(0) % tiles_per_seq) == 0
    ext_row = lax.broadcasted_iota(jnp.int32, (CONV_HALO + sub, 1), 0)
    keep = jnp.logical_or(ext_row >= CONV_HALO, jnp.logical_not(seq_start))
    cw, cb = cw_ref[...], cb_ref[...]

    def up(s):
        h = _rms(x_ref[s * sub:(s + 1) * sub, :], gpre)
        if s == 0:
            before = _rms(halo_ref[...], gpre)
        else:
            before = _rms(x_ref[s * sub - CONV_HALO:s * sub, :], gpre)
        hb_ext = jnp.concatenate([before, h], axis=0).astype(BF16)
        gate = _dot(hb_ext, wup_ref[:, :d_ff])
        if s == 0:
            gate = jnp.where(keep, gate, 0.0)
        return gate, _dot(hb_ext[CONV_HALO:], wup_ref[:, d_ff:])

    def mix(gate, val):
        conv = cb + cw[2:3] * gate[CONV_HALO:]
        conv = conv + cw[1:2] * gate[CONV_HALO - 1:CONV_HALO - 1 + sub]
        conv = conv + cw[0:1] * gate[CONV_HALO - 2:CONV_HALO - 2 + sub]
        gelu = 0.5 * conv * (1.0 + jnp.tanh(math.sqrt(2.0 / math.pi) * (conv + 0.044715 * (conv * conv * conv))))
        return _dot((gelu * val).astype(BF16), wdown_ref[...])

    def finish(s, acc):
        rows = slice(s * sub, (s + 1) * sub)
        o_ref[rows, :] = x_ref[rows, :] + _rms(acc, gpost_ref[...])

    ahead, behind = up(0), None
    for s in range(n_sub):
        current, ahead = ahead, (up(s + 1) if s + 1 < n_sub else None)
        acc = mix(*current)
        if behind is not None:
            finish(*behind)
        behind = (s, acc)
    finish(*behind)


def _ffn(x, gpre, wup, cw, cb, wdown, gpost, seq):
    tokens, d = x.shape
    d_ff = wdown.shape[0]
    tm = WIDE_TILE
    tps = seq // tm
    halo_blocks = tm // CONV_HALO
    row = lambda i: (i, 0)
    halo_spec = pl.BlockSpec((CONV_HALO, d), lambda i: (jnp.maximum(i * halo_blocks - 1, 0), 0))
    vmem = 4 * tm * d * 4 + (wup.size + wdown.size) * 2 + 6 * tm * d * 4 + 16 * tm * d_ff * 4 + (8 << 20)
    return pl.pallas_call(
        functools.partial(_ffn_kernel, tiles_per_seq=tps),
        grid=(tokens // tm,),
        in_specs=[pl.BlockSpec((tm, d), row), halo_spec, _resident((1, d)), _resident(wup.shape),
                  _resident(cw.shape), _resident((1, d_ff)), _resident(wdown.shape), _resident((1, d))],
        out_specs=pl.BlockSpec((tm, d), row),
        out_shape=jax.ShapeDtypeStruct((tokens, d), F32),
        compiler_params=_params(1, vmem),
        name="conv_ffn",
    )(x, x, gpre, wup, cw, cb, wdown, gpost)


def _att_proj_kernel(x_ref, gain_ref, w_ref, *refs, dilations, n_rope):
    n_tab = 2 * N_GROUPS
    tables, outs = refs[:n_tab], refs[n_tab:n_tab + len(dilations)]
    stage_ref, perm_ref = refs[n_tab + len(dilations):]
    tm, d = x_ref.shape
    h = _rms(x_ref[...], gain_ref[...])
    lhs = {1: h.astype(BF16)}
    for j in range(d // LANES):
        stage_ref[j] = h[:, j * LANES:(j + 1) * LANES]
    for pi, dil in enumerate(sorted(set(dilations) - {1})):
        per = tm // dil
        for j in range(d // LANES):
            for r in range(dil):
                piece = stage_ref[j, pl.ds(r, per, stride=dil), :]
                perm_ref[pi, r * per:(r + 1) * per, j * LANES:(j + 1) * LANES] = piece.astype(BF16)
        lhs[dil] = perm_ref.at[pi]
    work = [(s, c0) for s in sorted(range(len(outs)), key=lambda i: dilations[i])
            for c0 in range(0, ATT_GROUP_WIDTH, COL_CHUNK)]

    def project(s, c0):
        dil = dilations[s]
        col = s * ATT_GROUP_WIDTH + c0
        return _dot(lhs[dil] if dil == 1 else lhs[dil][...], w_ref[:, col:col + COL_CHUNK])

    ahead = project(*work[0])
    for i, (s, c0) in enumerate(work):
        acc, ahead = ahead, (project(*work[i + 1]) if i + 1 < len(work) else None)
        o_ref, dil = outs[s], dilations[s]
        gi = DILATIONS.index(dil)
        cos, signed_sin = tables[2 * gi][...], tables[2 * gi + 1][...]
        per = tm // dil
        for j in range(COL_CHUNK // LANES):
            t = acc[:, j * LANES:(j + 1) * LANES]
            if s < n_rope:
                t = _partial_rotary(t, cos, signed_sin)
            t = t.astype(BF16)
            lane0 = c0 + j * LANES
            for r in range(dil):
                o_ref[r, :, lane0:lane0 + LANES] = t[r * per:(r + 1) * per]


def _att_proj(x, gain, w, tables, n_rope, batch, seq):
    tokens, d = x.shape
    n = w.shape[1]
    gw = ATT_GROUP_WIDTH
    dilations = DILATIONS * (n // (N_GROUPS * gw))
    n_perm = len(set(dilations) - {1})
    tm = TOKEN_TILE
    tps = seq // tm
    row = lambda i: (i, 0)
    tab = pl.BlockSpec((tm, LANES), lambda i: (i % tps, 0))
    plane = lambda dil: pl.BlockSpec((None, dil, tm // dil, gw), lambda i: (i // tps, 0, i % tps, 0))
    vmem = (2 * tm * d * 4 + w.size * 2 + 2 * tm * n * 2 + 8 * tm * COL_CHUNK * 4 + 2 * len(tables) * tm * LANES * 4
            + tm * d * 4 + n_perm * tm * d * 2 + (8 << 20))
    return pl.pallas_call(
        functools.partial(_att_proj_kernel, dilations=dilations, n_rope=n_rope),
        grid=(tokens // tm,),
        in_specs=[pl.BlockSpec((tm, d), row), _resident((1, d)), _resident(w.shape)] + [tab] * len(tables),
        out_specs=[plane(dil) for dil in dilations],
        out_shape=[jax.ShapeDtypeStruct((batch, dil, seq // dil, gw), BF16) for dil in dilations],
        scratch_shapes=[pltpu.VMEM((d // LANES, tm, LANES), F32), pltpu.VMEM((n_perm, tm, d), BF16)],
        compiler_params=_params(1, vmem),
        name="att_proj",
    )(x, gain, w, *tables)


def _dil_attn_kernel(q_ref, k_ref, v_ref, o_ref, m_ref, d_ref, bias_first_ref, bias_band_ref):
    n_planes, length = q_ref.shape[0], q_ref.shape[1]
    blk = ATT_BLOCK
    n_blocks = length // blk
    n_pairs = q_ref.shape[2] // LANES
    lane = lax.broadcasted_iota(jnp.int32, (blk, LANES), 1)
    low = lane < ATT_HEAD_DIM
    is_b, _ = _qk_lane_layout()
    edges = [0] + [i for i in range(1, LANES) if is_b[i] != is_b[i - 1]] + [LANES]
    q_first = None
    for lo, hi in zip(edges[:-1], edges[1:]):
        if not is_b[lo]:
            run = jnp.logical_and(lane >= lo, lane < hi)
            q_first = run if q_first is None else jnp.logical_or(q_first, run)
    zero = jnp.zeros((blk, LANES), BF16)

    def bias(n_keys):
        qi = lax.broadcasted_iota(jnp.int32, (2 * blk, n_keys), 0) % blk
        kj = lax.broadcasted_iota(jnp.int32, (2 * blk, n_keys), 1)
        dist = qi + (n_keys - blk) - kj
        return jnp.where(jnp.logical_and(dist >= 0, dist <= blk), 0.0, NEG_BIG).astype(F32)

    bias_first_ref[...] = bias(blk)
    bias_band_ref[...] = bias(2 * blk)

    def scores(plane, q_rows, k_rows, bias_ref, hp):
        cols = slice(hp * LANES, (hp + 1) * LANES)
        q2 = q_ref[plane, q_rows, cols]
        qq = jnp.concatenate([jnp.where(q_first, q2, zero), jnp.where(q_first, zero, q2)], axis=0)
        return _dot_nt(qq, k_ref[plane, k_rows, cols]) + bias_ref[...]

    def softmax(s):
        m = jnp.max(s, axis=-1, keepdims=True)
        return m, jnp.exp2(s - m).astype(BF16)

    def finish(plane, q_rows, k_rows, bias_ref, hp, mp):
        m, p = mp
        cols = slice(hp * LANES, (hp + 1) * LANES)
        ones = jnp.ones((bias_ref.shape[1], LANES), BF16)
        pv = _dot(p, jnp.concatenate([v_ref[plane, k_rows, cols], ones], axis=1))
        o_ref[plane, q_rows, cols] = jnp.where(low, pv[:blk, :LANES], pv[blk:, :LANES]).astype(BF16)
        for half in range(2):
            head = 2 * hp + half
            rows = slice(half * blk, (half + 1) * blk)
            m_ref[plane, q_rows, head:head + 1] = m[rows]
            d_ref[plane, q_rows, head:head + 1] = pv[rows, LANES + head:LANES + head + 1]

    def run(items):
        work = [(*item, hp) for item in items for hp in range(n_pairs)]
        for plane, q_rows, _, _ in items:
            m_ref[plane, q_rows, :] = jnp.zeros((blk, LANES), F32)
            d_ref[plane, q_rows, :] = jnp.ones((blk, LANES), F32)
        n = len(work)
        s_vals, mp_vals = {}, {}
        for t in range(-ATT_SKEW, n):
            if t + ATT_SKEW < n:
                s_vals[t + ATT_SKEW] = scores(*work[t + ATT_SKEW])
            mid = t + ATT_SKEW // 2
            if 0 <= mid < n and ATT_SKEW > 1:
                mp_vals[mid] = softmax(s_vals.pop(mid))
            if t >= 0:
                if ATT_SKEW <= 1:
                    mp_vals[t] = softmax(s_vals.pop(t))
                finish(*work[t], mp_vals.pop(t))

    run([(plane, pl.ds(0, blk), pl.ds(0, blk), bias_first_ref) for plane in range(n_planes)])
    n_band = n_blocks - 1
    unroll = max(u for u in range(1, ATT_UNROLL + 1) if n_band % u == 0) if n_band else 0
    for plane in range(n_planes if n_band else 0):
        def later_blocks(i, carry, plane=plane):
            items = []
            for j in range(unroll):
                start = (i * unroll + j + 1) * blk
                start = start if isinstance(start, int) else pl.multiple_of(start, blk)
                items.append((plane, pl.ds(start, blk), pl.ds(start - blk, 2 * blk), bias_band_ref))
            run(items)
            return carry

        if n_band == unroll:
            later_blocks(0, 0)
        else:
            lax.fori_loop(0, n_band // unroll, later_blocks, 0)


def _dil_attn(q, k, v):
    batch, dil, length, gw = q.shape
    planes = min(dil, max(1, ATT_MIN_BLOCKS * ATT_BLOCK // length))
    spec = lambda width: pl.BlockSpec((None, planes, length, width), lambda b, r: (b, r, 0, 0))
    blk = ATT_BLOCK
    vmem = (2 * 4 * planes * length * gw * 2 + 4 * planes * length * LANES * 4 + 3 * 2 * blk * 2 * blk * 4
            + 48 * blk * 2 * blk * 4 + (8 << 20))
    stat = jax.ShapeDtypeStruct((batch, dil, length, LANES), F32)
    return pl.pallas_call(
        _dil_attn_kernel,
        grid=(batch, dil // planes),
        in_specs=[spec(gw), spec(gw), spec(gw)],
        out_specs=[spec(gw), spec(LANES), spec(LANES)],
        out_shape=[jax.ShapeDtypeStruct((batch, dil, length, gw), BF16), stat, stat],
        scratch_shapes=[pltpu.VMEM((2 * blk, blk), F32), pltpu.VMEM((2 * blk, 2 * blk), F32)],
        compiler_params=_params(2, vmem),
        name=f"dil_attn_d{dil}",
    )(q, k, v)


def kernel(x, ret_w_in, ret_w_out, ret_gn_gain, kv_norm, att_w_kv, att_w_q, att_w_o, norm_mix_pre, norm_mix_post, norm_ffn_pre, norm_ffn_post, ffn_w_up, ffn_conv_w, ffn_conv_b, ffn_w_down):
    batch, seq, d = x.shape
    depth = norm_mix_pre.shape[0]
    n_ret = ret_w_in.shape[0]
    assert seq % TOKEN_TILE == 0 and seq % WIDE_TILE == 0 and seq % RET_CHUNK == 0
    assert all(seq % (dil * ATT_BLOCK) == 0 and TOKEN_TILE % (16 * dil) == 0 for dil in DILATIONS)
    assert all(win // dil == ATT_BLOCK for win, dil in DIL_GROUPS)

    rcos, rsin, att_tables = _rope_tables(seq)
    q_scale = ATT_HEAD_DIM ** -0.5 * math.log2(math.e)
    n_qk = N_GROUPS * ATT_GROUP_WIDTH
    log_gamma = jnp.log(1.0 - 2.0 ** (-5.0 - jnp.arange(RET_HEADS, dtype=F32)))
    gain = lambda g: g[None, :]
    bf = lambda w: w.astype(BF16)

    xt = x.reshape(batch * seq, d)
    keys = values = None
    for layer in range(depth):
        if layer < n_ret:
            y = _ret_mixer(xt, gain(norm_mix_pre[layer]), bf(ret_w_in[layer]), rcos, rsin,
                           gain(ret_gn_gain[layer]), log_gamma, batch, seq)
            xt = _outproj(xt, y, bf(ret_w_out[layer]), gain(norm_mix_post[layer]))
        else:
            bi = layer - n_ret
            queries = _att_proj(xt, gain(norm_mix_pre[layer]), bf(_qk_slab_columns(att_w_q[bi]) * q_scale),
                                att_tables, N_GROUPS, batch, seq)
            outs, maxes, dens = zip(*[_dil_attn(queries[gi], keys[gi], values[gi]) for gi in range(N_GROUPS)])
            xt = _att_merge(xt, outs, maxes, dens, bf(att_w_o[bi]), gain(norm_mix_post[layer]), seq)
        xt = _ffn(xt, gain(norm_ffn_pre[layer]), bf(ffn_w_up[layer]), ffn_conv_w[layer], gain(ffn_conv_b[layer]),
                  bf(ffn_w_down[layer]), gain(norm_ffn_post[layer]), seq)
        if layer == n_ret - 1:
            w_kv = jnp.concatenate([_qk_slab_columns(att_w_kv[:, :n_qk]), att_w_kv[:, n_qk:]], axis=1)
            kv = _att_proj(xt, gain(kv_norm), bf(w_kv), att_tables, N_GROUPS, batch, seq)
            keys, values = kv[:N_GROUPS], kv[N_GROUPS:]
    return xt.reshape(batch, seq, d)
```

```python
import functools
import math

import jax
import jax.numpy as jnp
import numpy as np
from jax import lax
from jax.experimental import pallas as pl
from jax.experimental.pallas import tpu as pltpu

F32 = jnp.float32
BF16 = jnp.bfloat16

LANES = 128
V7X_VMEM_BYTES = 64 * 1024 * 1024

RMS_EPS = 1e-6
GN_EPS = 1e-6
RET_HEADS = 4
RET_QK_DIM = 256
RET_V_DIM = 512
RET_ROT_BASE = 10000.0
ATT_HEADS = 16
ATT_HEAD_DIM = 64
ATT_GROUP_WIDTH = ATT_HEADS * ATT_HEAD_DIM
DIL_GROUPS = ((128, 1), (512, 4), (2048, 16))
DILATIONS = tuple(d for _, d in DIL_GROUPS)
N_GROUPS = 3
ATT_BLOCK = 128
ROPE_THETA = 500000.0
ROPE_DIMS = ATT_HEAD_DIM // 4
CONV_WIDTH = 3

TOKEN_TILE = 512
WIDE_TILE = 1024
COL_CHUNK = 512
FFN_SUB = 512
MERGE_SUB = 512
CONV_HALO = 16
RET_CHUNK = 256
RET_SKEW = 0
RET_PROJ_ROWS = 256
ATT_MIN_BLOCKS = 16
ATT_SKEW = 4
ATT_UNROLL = 15
NEG_BIG = -1e30


def _vmem_limit(nbytes):
    return int(min(V7X_VMEM_BYTES - (4 << 20), max(nbytes, 16 << 20)))


def _params(n_axes, vmem_bytes):
    return pltpu.CompilerParams(dimension_semantics=("arbitrary",) * n_axes,
                                vmem_limit_bytes=_vmem_limit(vmem_bytes))


def _resident(shape, layer=None):
    if layer is None:
        return pl.BlockSpec(shape, lambda *_: (0,) * len(shape), pipeline_mode=pl.Buffered(1))
    return pl.BlockSpec((None,) + tuple(shape[1:]), lambda *_: (layer,) + (0,) * (len(shape) - 1),
                        pipeline_mode=pl.Buffered(1))


def _rms(x, gain):
    return x * lax.rsqrt(jnp.mean(x * x, axis=-1, keepdims=True) + RMS_EPS) * gain


def _dot(a, b):
    return jnp.dot(a, b, preferred_element_type=F32)


def _dot_nt(a, b):
    return lax.dot_general(a, b, (((1,), (1,)), ((), ())), preferred_element_type=F32)


def _qk_lane_layout():
    half = ROPE_DIMS // 2
    lane = np.arange(LANES)
    upper = lane >= ATT_HEAD_DIM
    local = lane - ATT_HEAD_DIM * upper
    is_b = np.where(upper, local >= half, (local >= half) & (local < ROPE_DIMS))
    dim = np.where(local < ROPE_DIMS, local % half + half * upper, local)
    dim = np.where(upper & (local >= ROPE_DIMS), local, dim)
    return is_b.astype(np.int32), dim.astype(np.int32)


def _qk_slab_columns(w):
    is_b, dim = _qk_lane_layout()
    heads = w.reshape(*w.shape[:-1], -1, 2, ATT_HEAD_DIM)
    starts = [0] + [i for i in range(1, LANES) if is_b[i] != is_b[i - 1] or dim[i] != dim[i - 1] + 1] + [LANES]
    runs = [heads[..., int(is_b[lo]), int(dim[lo]):int(dim[lo]) + hi - lo] for lo, hi in zip(starts[:-1], starts[1:])]
    return jnp.concatenate(runs, axis=-1).reshape(w.shape)


def _rope_tables_kernel(rfreq_ref, afreq_ref, asign_ref, rcos_ref, rsin_ref, *att_refs):
    seq = rcos_ref.shape[0]
    row = lax.broadcasted_iota(jnp.int32, (seq, LANES), 0)
    rang = row.astype(F32) * rfreq_ref[...]
    rcos_ref[...] = jnp.cos(rang)
    rsin_ref[...] = jnp.sin(rang)
    tm = TOKEN_TILE
    for i, dil in enumerate(DILATIONS):
        per = tm // dil
        in_tile = row & (tm - 1)
        pos = (row - in_tile) + (in_tile & (per - 1)) * dil + lax.shift_right_logical(in_tile, per.bit_length() - 1)
        ang = pos.astype(F32) * afreq_ref[...]
        att_refs[2 * i][...] = jnp.cos(ang)
        att_refs[2 * i + 1][...] = jnp.sin(ang) * asign_ref[...]


def _rope_tables(seq):
    rfreq = 1.0 / (RET_ROT_BASE ** jnp.linspace(0.0, 1.0, RET_QK_DIM // 2, dtype=F32))
    inv = ROPE_THETA ** (-jnp.arange(0, ROPE_DIMS, 2, dtype=F32) / ROPE_DIMS)
    half = ROPE_DIMS // 2
    _, dim = _qk_lane_layout()
    afreq = jnp.where(dim < ROPE_DIMS, inv[dim % half], 0.0).astype(F32)
    asign = jnp.asarray(np.where(dim < half, -1.0, np.where(dim < ROPE_DIMS, 1.0, 0.0)), F32)
    tab = jax.ShapeDtypeStruct((seq, LANES), F32)
    tabs = pl.pallas_call(
        _rope_tables_kernel,
        out_shape=(tab,) * (2 + 2 * N_GROUPS),
        name="rope_tables",
    )(rfreq[None, :], afreq[None, :], asign[None, :])
    return tabs[0], tabs[1], tabs[2:]


def _partial_rotary(t, cos, signed_sin):
    return t * cos + pltpu.roll(t, ATT_HEAD_DIM, 1) * signed_sin


def _ret_mixer_kernel(lg_ref, x_ref, gain_ref, wq_ref, wk_ref, wv_ref, wg_ref, cos_ref, sin_ref, gn_ref, y_ref,
                      state_ref):
    head = pl.program_id(1)
    lg = lg_ref[head]
    seq = x_ref.shape[0]
    pt, ck = RET_PROJ_ROWS, RET_CHUNK
    half = RET_QK_DIM // 2

    n1 = (lax.broadcasted_iota(jnp.int32, (pt, 1), 0) % ck + 1).astype(F32)
    q_scale = jnp.exp(n1 * lg)
    k_scale = jnp.exp(-n1 * lg) * RET_QK_DIM ** -0.5
    causal = lax.broadcasted_iota(jnp.int32, (ck, ck), 0) >= lax.broadcasted_iota(jnp.int32, (ck, ck), 1)
    chunk_decay = jnp.exp(jnp.full((1, RET_V_DIM), ck, F32) * lg)
    gn = gn_ref[...]

    def project(t):
        rows = slice(t * pt, (t + 1) * pt)
        hb = _rms(x_ref[rows, :], gain_ref[...]).astype(BF16)
        cos, sin = cos_ref[rows, :], sin_ref[rows, :]

        def rotated(acc, scale):
            a, b = acc[:, :half], acc[:, half:]
            return (a * cos - b * sin) * scale, (b * cos + a * sin) * scale

        q_lo, q_hi = rotated(_dot(hb, wq_ref[...]), q_scale)
        k_lo, k_hi = rotated(_dot(hb, wk_ref[...]), k_scale)
        q = jnp.concatenate([q_lo, q_hi], axis=1).astype(BF16)
        kt = jnp.concatenate([k_lo.T, k_hi.T], axis=0).astype(BF16)
        v = _dot(hb, wv_ref[...]).astype(BF16)
        gate = _dot(hb, wg_ref[...])
        g = (gate / (1.0 + jnp.exp(-gate)) * gn).astype(BF16)
        return q, kt, v, g

    def state_free(c, q, kt, v, g, j):
        rows = slice(j * ck, (j + 1) * ck)
        qc, ktc, vc = q[rows], kt[:, rows], v[rows]
        scores = jnp.where(causal, _dot(qc, ktc), 0.0)
        intra = _dot(scores.astype(BF16), vc)
        upd = _dot(ktc, vc) if (c + 1) * ck < seq else None
        return c, qc, intra, upd, g[rows]

    def finish(c, qc, intra, upd, gc):
        y = intra + _dot(qc, state_ref[...].astype(BF16))
        if upd is not None:
            state_ref[...] = (state_ref[...] + upd) * chunk_decay
        mu = jnp.mean(y, axis=-1, keepdims=True)
        yc = y - mu
        var = jnp.mean(yc * yc, axis=-1, keepdims=True)
        yn = (yc * lax.rsqrt(var + GN_EPS)).astype(BF16)
        y_ref[c * ck:(c + 1) * ck, :] = yn * gc

    state_ref[...] = jnp.zeros_like(state_ref)
    n_tiles = seq // pt
    ahead, pending = project(0), []
    for t in range(n_tiles):
        current, ahead = ahead, (project(t + 1) if t + 1 < n_tiles else None)
        for j in range(pt // ck):
            pending.append(state_free(t * (pt // ck) + j, *current, j))
            if len(pending) > RET_SKEW:
                finish(*pending.pop(0))
    for item in pending:
        finish(*item)


def _ret_mixer(x, gain, w, layer, cos, sin, gn_gain, log_gamma, batch, seq):
    tokens, d = x.shape
    dqk = RET_HEADS * RET_QK_DIM
    dv = RET_HEADS * RET_V_DIM
    assert seq % RET_PROJ_ROWS == 0 and RET_PROJ_ROWS % RET_CHUNK == 0
    qk_blocks = dqk // RET_QK_DIM
    v_blocks = dv // RET_V_DIM
    wq = pl.BlockSpec((None, d, RET_QK_DIM), lambda b, h: (layer, 0, h))
    wk = pl.BlockSpec((None, d, RET_QK_DIM), lambda b, h: (layer, 0, qk_blocks + h))
    wv = pl.BlockSpec((None, d, RET_V_DIM), lambda b, h: (layer, 0, 2 * dqk // RET_V_DIM + h))
    wg = pl.BlockSpec((None, d, RET_V_DIM), lambda b, h: (layer, 0, 2 * dqk // RET_V_DIM + v_blocks + h))
    head_cols = pl.BlockSpec((seq, RET_V_DIM), lambda b, h: (b, h))
    vmem = (2 * seq * d * 4 + 2 * d * (2 * RET_QK_DIM + 2 * RET_V_DIM) * 2 + 2 * seq * RET_V_DIM * 2
            + 4 * seq * LANES * 4 + 16 * RET_PROJ_ROWS * RET_V_DIM * 4 + (8 << 20))
    return pl.pallas_call(
        _ret_mixer_kernel,
        grid=(batch, RET_HEADS),
        in_specs=[pl.BlockSpec(memory_space=pltpu.SMEM), pl.BlockSpec((seq, d), lambda b, h: (b, 0)),
                  _resident((1, d)), wq, wk, wv, wg, _resident((seq, LANES)), _resident((seq, LANES)),
                  pl.BlockSpec((1, RET_V_DIM), lambda b, h: (0, h))],
        out_specs=head_cols,
        out_shape=jax.ShapeDtypeStruct((tokens, dv), BF16),
        scratch_shapes=[pltpu.VMEM((RET_QK_DIM, RET_V_DIM), F32)],
        compiler_params=_params(2, vmem),
        name="ret_mixer",
    )(log_gamma, x, gain, w, w, w, w, cos, sin, gn_gain)


def _outproj_kernel(x_ref, a_ref, w_ref, gain_ref, o_ref):
    m = _dot(a_ref[...], w_ref[...])
    o_ref[...] = x_ref[...] + _rms(m, gain_ref[...])


def _outproj(x, a, w, layer, gain):
    tokens, d = x.shape
    kdim = a.shape[1]
    tm = WIDE_TILE
    row = lambda i: (i, 0)
    vmem = 4 * tm * d * 4 + 2 * tm * kdim * 2 + kdim * d * 2 + 4 * tm * d * 4 + (8 << 20)
    return pl.pallas_call(
        _outproj_kernel,
        grid=(tokens // tm,),
        in_specs=[pl.BlockSpec((tm, d), row), pl.BlockSpec((tm, kdim), row), _resident(w.shape, layer),
                  _resident((1, d))],
        out_specs=pl.BlockSpec((tm, d), row),
        out_shape=jax.ShapeDtypeStruct((tokens, d), F32),
        compiler_params=_params(1, vmem),
        name="outproj",
    )(x, a, w, gain)


def _natural_rows(src_ref, row0, rows, lane0, stage_ref, slab, dilation):
    per, first = rows // dilation, row0 // dilation
    if dilation == 1:
        return src_ref[0, first:first + per, lane0:lane0 + LANES].astype(F32)
    for r in range(dilation):
        piece = src_ref[r, first:first + per, lane0:lane0 + LANES]
        stage_ref[slab, pl.ds(r, per, stride=dilation), :] = piece.astype(F32)
    return stage_ref[slab]


def _att_merge_kernel(x_ref, o0_ref, o1_ref, o2_ref, m0_ref, m1_ref, m2_ref, d0_ref, d1_ref, d2_ref,
                      expand_ref, w_ref, gain_ref, out_ref, stage_ref, merged_ref):
    o_refs = (o0_ref, o1_ref, o2_ref)
    tm = x_ref.shape[0]
    sub = min(MERGE_SUB, tm)
    n_sub = tm // sub
    n_slabs = ATT_GROUP_WIDTH // LANES
    per_sub = stage_ref.shape[0] // n_sub

    def weights(s):
        nat = lambda ref, k, g: _natural_rows(ref, s * sub, sub, 0, stage_ref, s * per_sub + k, DILATIONS[g])
        ms = [nat(r, g, g) for g, r in enumerate((m0_ref, m1_ref, m2_ref))]
        dens = [nat(r, N_GROUPS + g, g) for g, r in enumerate((d0_ref, d1_ref, d2_ref))]
        top = jnp.maximum(jnp.maximum(ms[0], ms[1]), ms[2])
        es = [jnp.exp2(m - top) for m in ms]
        inv = 1.0 / (es[0] * dens[0] + es[1] * dens[1] + es[2] * dens[2])
        lane = lax.broadcasted_iota(jnp.int32, top.shape, 1)
        packed = None
        for g, e in enumerate(es):
            alpha = jnp.where(lane < ATT_HEADS, e * inv, 0.0)
            hi = alpha.astype(BF16).astype(F32)
            lo = alpha - hi
            for part, piece in enumerate((hi, lo)):
                shift = (2 * g + part) * ATT_HEADS
                piece = pltpu.roll(piece, shift, 1) if shift else piece
                packed = piece if packed is None else packed + piece
        return _dot(packed.astype(BF16), expand_ref[...])

    def finish(s, wide):
        rows = slice(s * sub, (s + 1) * sub)
        for j in range(n_slabs):
            acc = None
            for g in range(N_GROUPS):
                slab = s * per_sub + 2 * N_GROUPS + g * n_slabs + j
                o = _natural_rows(o_refs[g], s * sub, sub, j * LANES, stage_ref, slab, DILATIONS[g])
                c0 = g * ATT_GROUP_WIDTH + j * LANES
                term = wide[:, c0:c0 + LANES] * o
                acc = term if acc is None else acc + term
            merged_ref[rows, j * LANES:(j + 1) * LANES] = acc.astype(BF16)
        m = _dot(merged_ref[rows, :], w_ref[...])
        out_ref[rows, :] = x_ref[rows, :] + _rms(m, gain_ref[...])

    ahead = weights(0)
    for s in range(n_sub):
        current, ahead = ahead, (weights(s + 1) if s + 1 < n_sub else None)
        finish(s, current)


def _att_merge(x, outs, maxes, dens, w, layer, gain, seq):
    tokens, d = x.shape
    tm = WIDE_TILE
    sub = min(MERGE_SUB, tm)
    tps = seq // tm
    row = lambda i: (i, 0)
    gw = ATT_GROUP_WIDTH

    def plane(width, dil):
        return pl.BlockSpec((None, dil, tm // dil, width), lambda i: (i // tps, 0, i % tps, 0))

    col = jnp.arange(N_GROUPS * gw)
    src = 2 * ATT_HEADS * (col // gw) + (col % gw) // ATT_HEAD_DIM
    lane = jnp.arange(LANES)[:, None]
    expand = ((lane == src[None, :]) | (lane == src[None, :] + ATT_HEADS)).astype(BF16)
    n_stage = 2 * N_GROUPS + N_GROUPS * (gw // LANES)
    vmem = (4 * tm * d * 4 + 6 * tm * gw * 2 + 12 * tm * LANES * 4 + gw * d * 2 + expand.size * 2
            + n_stage * tm * LANES * 4 + 8 * tm * d * 4 + (8 << 20))
    return pl.pallas_call(
        _att_merge_kernel,
        grid=(tokens // tm,),
        in_specs=[pl.BlockSpec((tm, d), row)] + [plane(gw, dil) for dil in DILATIONS]
                 + [plane(LANES, dil) for dil in DILATIONS] * 2
                 + [_resident(expand.shape), _resident(w.shape, layer), _resident((1, d))],
        out_specs=pl.BlockSpec((tm, d), row),
        out_shape=jax.ShapeDtypeStruct((tokens, d), F32),
        scratch_shapes=[pltpu.VMEM((n_stage * (tm // sub), sub, LANES), F32), pltpu.VMEM((tm, gw), BF16)],
        compiler_params=_params(1, vmem),
        name="att_merge",
    )(x, *outs, *maxes, *dens, expand, w, gain)


def _ffn_kernel(x_ref, halo_ref, gpre_ref, wup_ref, cw_ref, cb_ref, wdown_ref, gpost_ref, o_ref, *, tiles_per_seq):
    tm = x_ref.shape[0]
    sub = min(FFN_SUB, tm)
    n_sub = tm // sub
    d_ff = wdown_ref.shape[0]
    gpre = gpre_ref[...]
    seq_start = (pl.program_id(0) % tiles_per_seq) == 0
    ext_row = lax.broadcasted_iota(jnp.int32, (CONV_HALO + sub, 1), 0)
    keep = jnp.logical_or(ext_row >= CONV_HALO, jnp.logical_not(seq_start))
    cw, cb = cw_ref[...], cb_ref[...]

    def up(s):
        h = _rms(x_ref[s * sub:(s + 1) * sub, :], gpre)
        if s == 0:
            before = _rms(halo_ref[...], gpre)
        else:
            before = _rms(x_ref[s * sub - CONV_HALO:s * sub, :], gpre)
        hb_ext = jnp.concatenate([before, h], axis=0).astype(BF16)
        gate = _dot(hb_ext, wup_ref[:, :d_ff])
        if s == 0:
            gate = jnp.where(keep, gate, 0.0)
        return gate, _dot(hb_ext[CONV_HALO:], wup_ref[:, d_ff:])

    def mix(gate, val):
        conv = cb + cw[2:3] * gate[CONV_HALO:]
        conv = conv + cw[1:2] * gate[CONV_HALO - 1:CONV_HALO - 1 + sub]
        conv = conv + cw[0:1] * gate[CONV_HALO - 2:CONV_HALO - 2 + sub]
        gelu = 0.5 * conv * (1.0 + jnp.tanh(math.sqrt(2.0 / math.pi) * (conv + 0.044715 * (conv * conv * conv))))
        return _dot((gelu * val).astype(BF16), wdown_ref[...])

    def finish(s, acc):
        rows = slice(s * sub, (s + 1) * sub)
        o_ref[rows, :] = x_ref[rows, :] + _rms(acc, gpost_ref[...])

    ahead, behind = up(0), None
    for s in range(n_sub):
        current, ahead = ahead, (up(s + 1) if s + 1 < n_sub else None)
        acc = mix(*current)
        if behind is not None:
            finish(*behind)
        behind = (s, acc)
    finish(*behind)


def _ffn(x, gpre, wup, cw, cb, wdown, layer, gpost, seq):
    tokens, d = x.shape
    d_ff = wdown.shape[1]
    tm = WIDE_TILE
    tps = seq // tm
    halo_blocks = tm // CONV_HALO
    row = lambda i: (i, 0)
    halo_spec = pl.BlockSpec((CONV_HALO, d), lambda i: (jnp.maximum(i * halo_blocks - 1, 0), 0))
    vmem = 4 * tm * d * 4 + 3 * d * d_ff * 2 + 6 * tm * d * 4 + 16 * tm * d_ff * 4 + (8 << 20)
    return pl.pallas_call(
        functools.partial(_ffn_kernel, tiles_per_seq=tps),
        grid=(tokens // tm,),
        in_specs=[pl.BlockSpec((tm, d), row), halo_spec, _resident((1, d)), _resident(wup.shape, layer),
                  _resident(cw.shape), _resident((1, d_ff)), _resident(wdown.shape, layer), _resident((1, d))],
        out_specs=pl.BlockSpec((tm, d), row),
        out_shape=jax.ShapeDtypeStruct((tokens, d), F32),
        compiler_params=_params(1, vmem),
        name="conv_ffn",
    )(x, x, gpre, wup, cw, cb, wdown, gpost)


def _att_proj_kernel(x_ref, gain_ref, w_ref, *refs, dilations, n_rope):
    n_tab = 2 * N_GROUPS
    tables, outs = refs[:n_tab], refs[n_tab:n_tab + len(dilations)]
    stage_ref, perm_ref = refs[n_tab + len(dilations):]
    tm, d = x_ref.shape
    h = _rms(x_ref[...], gain_ref[...])
    lhs = {1: h.astype(BF16)}
    for j in range(d // LANES):
        stage_ref[j] = h[:, j * LANES:(j + 1) * LANES]
    for pi, dil in enumerate(sorted(set(dilations) - {1})):
        per = tm // dil
        for j in range(d // LANES):
            for r in range(dil):
                piece = stage_ref[j, pl.ds(r, per, stride=dil), :]
                perm_ref[pi, r * per:(r + 1) * per, j * LANES:(j + 1) * LANES] = piece.astype(BF16)
        lhs[dil] = perm_ref.at[pi]
    work = [(s, c0) for s in sorted(range(len(outs)), key=lambda i: dilations[i])
            for c0 in range(0, ATT_GROUP_WIDTH, COL_CHUNK)]

    def project(s, c0):
        dil = dilations[s]
        col = s * ATT_GROUP_WIDTH + c0
        return _dot(lhs[dil] if dil == 1 else lhs[dil][...], w_ref[:, col:col + COL_CHUNK])

    ahead = project(*work[0])
    for i, (s, c0) in enumerate(work):
        acc, ahead = ahead, (project(*work[i + 1]) if i + 1 < len(work) else None)
        o_ref, dil = outs[s], dilations[s]
        gi = DILATIONS.index(dil)
        cos, signed_sin = tables[2 * gi][...], tables[2 * gi + 1][...]
        per = tm // dil
        for j in range(COL_CHUNK // LANES):
            t = acc[:, j * LANES:(j + 1) * LANES]
            if s < n_rope:
                t = _partial_rotary(t, cos, signed_sin)
            t = t.astype(BF16)
            lane0 = c0 + j * LANES
            for r in range(dil):
                o_ref[r, :, lane0:lane0 + LANES] = t[r * per:(r + 1) * per]


def _att_proj(x, gain, w, layer, tables, n_rope, batch, seq):
    tokens, d = x.shape
    n = w.shape[-1]
    gw = ATT_GROUP_WIDTH
    dilations = DILATIONS * (n // (N_GROUPS * gw))
    n_perm = len(set(dilations) - {1})
    tm = TOKEN_TILE
    tps = seq // tm
    row = lambda i: (i, 0)
    tab = pl.BlockSpec((tm, LANES), lambda i: (i % tps, 0))
    plane = lambda dil: pl.BlockSpec((None, dil, tm // dil, gw), lambda i: (i // tps, 0, i % tps, 0))
    vmem = (2 * tm * d * 4 + d * n * 2 + 2 * tm * n * 2 + 8 * tm * COL_CHUNK * 4 + 2 * len(tables) * tm * LANES * 4
            + tm * d * 4 + n_perm * tm * d * 2 + (8 << 20))
    return pl.pallas_call(
        functools.partial(_att_proj_kernel, dilations=dilations, n_rope=n_rope),
        grid=(tokens // tm,),
        in_specs=[pl.BlockSpec((tm, d), row), _resident((1, d)), _resident(w.shape, layer)] + [tab] * len(tables),
        out_specs=[plane(dil) for dil in dilations],
        out_shape=[jax.ShapeDtypeStruct((batch, dil, seq // dil, gw), BF16) for dil in dilations],
        scratch_shapes=[pltpu.VMEM((d // LANES, tm, LANES), F32), pltpu.VMEM((n_perm, tm, d), BF16)],
        compiler_params=_params(1, vmem),
        name="att_proj",
    )(x, gain, w, *tables)


def _dil_attn_kernel(q_ref, k_ref, v_ref, o_ref, m_ref, d_ref, bias_first_ref, bias_band_ref):
    n_planes, length = q_ref.shape[0], q_ref.shape[1]
    blk = ATT_BLOCK
    n_blocks = length // blk
    n_pairs = q_ref.shape[2] // LANES
    lane = lax.broadcasted_iota(jnp.int32, (blk, LANES), 1)
    low = lane < ATT_HEAD_DIM
    is_b, _ = _qk_lane_layout()
    edges = [0] + [i for i in range(1, LANES) if is_b[i] != is_b[i - 1]] + [LANES]
    q_first = None
    for lo, hi in zip(edges[:-1], edges[1:]):
        if not is_b[lo]:
            run = jnp.logical_and(lane >= lo, lane < hi)
            q_first = run if q_first is None else jnp.logical_or(q_first, run)
    zero = jnp.zeros((blk, LANES), BF16)

    def bias(n_keys):
        qi = lax.broadcasted_iota(jnp.int32, (2 * blk, n_keys), 0) % blk
        kj = lax.broadcasted_iota(jnp.int32, (2 * blk, n_keys), 1)
        dist = qi + (n_keys - blk) - kj
        return jnp.where(jnp.logical_and(dist >= 0, dist <= blk), 0.0, NEG_BIG).astype(F32)

    bias_first_ref[...] = bias(blk)
    bias_band_ref[...] = bias(2 * blk)

    def scores(plane, q_rows, k_rows, bias_ref, hp):
        cols = slice(hp * LANES, (hp + 1) * LANES)
        q2 = q_ref[plane, q_rows, cols]
        qq = jnp.concatenate([jnp.where(q_first, q2, zero), jnp.where(q_first, zero, q2)], axis=0)
        return _dot_nt(qq, k_ref[plane, k_rows, cols]) + bias_ref[...]

    def softmax(s):
        m = jnp.max(s, axis=-1, keepdims=True)
        return m, jnp.exp2(s - m).astype(BF16)

    def finish(plane, q_rows, k_rows, bias_ref, hp, mp):
        m, p = mp
        cols = slice(hp * LANES, (hp + 1) * LANES)
        ones = jnp.ones((bias_ref.shape[1], LANES), BF16)
        pv = _dot(p, jnp.concatenate([v_ref[plane, k_rows, cols], ones], axis=1))
        o_ref[plane, q_rows, cols] = jnp.where(low, pv[:blk, :LANES], pv[blk:, :LANES]).astype(BF16)
        for half in range(2):
            head = 2 * hp + half
            rows = slice(half * blk, (half + 1) * blk)
            m_ref[plane, q_rows, head:head + 1] = m[rows]
            d_ref[plane, q_rows, head:head + 1] = pv[rows, LANES + head:LANES + head + 1]

    def run(items):
        work = [(*item, hp) for item in items for hp in range(n_pairs)]
        for plane, q_rows, _, _ in items:
            m_ref[plane, q_rows, :] = jnp.zeros((blk, LANES), F32)
            d_ref[plane, q_rows, :] = jnp.ones((blk, LANES), F32)
        n = len(work)
        s_vals, mp_vals = {}, {}
        for t in range(-ATT_SKEW, n):
            if t + ATT_SKEW < n:
                s_vals[t + ATT_SKEW] = scores(*work[t + ATT_SKEW])
            mid = t + ATT_SKEW // 2
            if 0 <= mid < n and ATT_SKEW > 1:
                mp_vals[mid] = softmax(s_vals.pop(mid))
            if t >= 0:
                if ATT_SKEW <= 1:
                    mp_vals[t] = softmax(s_vals.pop(t))
                finish(*work[t], mp_vals.pop(t))

    run([(plane, pl.ds(0, blk), pl.ds(0, blk), bias_first_ref) for plane in range(n_planes)])
    n_band = n_blocks - 1
    unroll = max(u for u in range(1, ATT_UNROLL + 1) if n_band % u == 0) if n_band else 0
    for plane in range(n_planes if n_band else 0):
        def later_blocks(i, carry, plane=plane):
            items = []
            for j in range(unroll):
                start = (i * unroll + j + 1) * blk
                start = start if isinstance(start, int) else pl.multiple_of(start, blk)
                items.append((plane, pl.ds(start, blk), pl.ds(start - blk, 2 * blk), bias_band_ref))
            run(items)
            return carry

        if n_band == unroll:
            later_blocks(0, 0)
        else:
            lax.fori_loop(0, n_band // unroll, later_blocks, 0)


def _dil_attn(q, k, v):
    batch, dil, length, gw = q.shape
    planes = min(dil, max(1, ATT_MIN_BLOCKS * ATT_BLOCK // length))
    spec = lambda width: pl.BlockSpec((None, planes, length, width), lambda b, r: (b, r, 0, 0))
    blk = ATT_BLOCK
    vmem = (2 * 4 * planes * length * gw * 2 + 4 * planes * length * LANES * 4 + 3 * 2 * blk * 2 * blk * 4
            + 48 * blk * 2 * blk * 4 + (8 << 20))
    stat = jax.ShapeDtypeStruct((batch, dil, length, LANES), F32)
    return pl.pallas_call(
        _dil_attn_kernel,
        grid=(batch, dil // planes),
        in_specs=[spec(gw), spec(gw), spec(gw)],
        out_specs=[spec(gw), spec(LANES), spec(LANES)],
        out_shape=[jax.ShapeDtypeStruct((batch, dil, length, gw), BF16), stat, stat],
        scratch_shapes=[pltpu.VMEM((2 * blk, blk), F32), pltpu.VMEM((2 * blk, 2 * blk), F32)],
        compiler_params=_params(2, vmem),
        name=f"dil_attn_d{dil}",
    )(q, k, v)


def kernel(x, ret_w_in, ret_w_out, ret_gn_gain, kv_norm, att_w_kv, att_w_q, att_w_o, norm_mix_pre, norm_mix_post, norm_ffn_pre, norm_ffn_post, ffn_w_up, ffn_conv_w, ffn_conv_b, ffn_w_down):
    batch, seq, d = x.shape
    depth = norm_mix_pre.shape[0]
    n_ret = ret_w_in.shape[0]
    assert seq % TOKEN_TILE == 0 and seq % WIDE_TILE == 0 and seq % RET_CHUNK == 0
    assert all(seq % (dil * ATT_BLOCK) == 0 and TOKEN_TILE % (16 * dil) == 0 for dil in DILATIONS)
    assert all(win // dil == ATT_BLOCK for win, dil in DIL_GROUPS)

    rcos, rsin, att_tables = _rope_tables(seq)
    q_scale = ATT_HEAD_DIM ** -0.5 * math.log2(math.e)
    n_qk = N_GROUPS * ATT_GROUP_WIDTH
    log_gamma = jnp.log(1.0 - 2.0 ** (-5.0 - jnp.arange(RET_HEADS, dtype=F32)))
    gain = lambda g: g[None, :]
    bf = lambda w: w.astype(BF16)
    w_in, w_out, w_o = bf(ret_w_in), bf(ret_w_out), bf(att_w_o)
    w_q = bf(_qk_slab_columns(att_w_q) * q_scale)
    w_kv = bf(jnp.concatenate([_qk_slab_columns(att_w_kv[:, :n_qk]), att_w_kv[:, n_qk:]], axis=1))
    w_up, w_down = bf(ffn_w_up), bf(ffn_w_down)

    xt = x.reshape(batch * seq, d)
    keys = values = None
    for layer in range(depth):
        if layer < n_ret:
            y = _ret_mixer(xt, gain(norm_mix_pre[layer]), w_in, layer, rcos, rsin,
                           gain(ret_gn_gain[layer]), log_gamma, batch, seq)
            xt = _outproj(xt, y, w_out, layer, gain(norm_mix_post[layer]))
        else:
            bi = layer - n_ret
            queries = _att_proj(xt, gain(norm_mix_pre[layer]), w_q, bi, att_tables, N_GROUPS, batch, seq)
            outs, maxes, dens = zip(*[_dil_attn(queries[gi], keys[gi], values[gi]) for gi in range(N_GROUPS)])
            xt = _att_merge(xt, outs, maxes, dens, w_o, bi, gain(norm_mix_post[layer]), seq)
        xt = _ffn(xt, gain(norm_ffn_pre[layer]), w_up, ffn_conv_w[layer], gain(ffn_conv_b[layer]),
                  w_down, layer, gain(norm_ffn_post[layer]), seq)
        if layer == n_ret - 1:
            kv = _att_proj(xt, gain(kv_norm), w_kv, None, att_tables, N_GROUPS, batch, seq)
            keys, values = kv[:N_GROUPS], kv[N_GROUPS:]
    return xt.reshape(batch, seq, d)
```

```python
import functools
import math

import jax
import jax.numpy as jnp
import numpy as np
from jax import lax
from jax.experimental import pallas as pl
from jax.experimental.pallas import tpu as pltpu

F32 = jnp.float32
BF16 = jnp.bfloat16

LANES = 128
V7X_VMEM_BYTES = 64 * 1024 * 1024

RMS_EPS = 1e-6
GN_EPS = 1e-6
RET_HEADS = 4
RET_QK_DIM = 256
RET_V_DIM = 512
RET_ROT_BASE = 10000.0
ATT_HEADS = 16
ATT_HEAD_DIM = 64
ATT_GROUP_WIDTH = ATT_HEADS * ATT_HEAD_DIM
DIL_GROUPS = ((128, 1), (512, 4), (2048, 16))
DILATIONS = tuple(d for _, d in DIL_GROUPS)
N_GROUPS = 3
ATT_BLOCK = 128
ROPE_THETA = 500000.0
ROPE_DIMS = ATT_HEAD_DIM // 4
CONV_WIDTH = 3

TOKEN_TILE = 512
WIDE_TILE = 1024
COL_CHUNK = 512
FFN_SUB = 512
MERGE_SUB = 512
CONV_HALO = 16
RET_CHUNK = 256
RET_SKEW = 0
RET_PROJ_ROWS = 256
ATT_MIN_BLOCKS = 16
ATT_SKEW = 4
ATT_UNROLL = 15
NEG_BIG = -1e30


def _vmem_limit(nbytes):
    return int(min(V7X_VMEM_BYTES - (4 << 20), max(nbytes, 16 << 20)))


def _params(n_axes, vmem_bytes):
    return pltpu.CompilerParams(dimension_semantics=("arbitrary",) * n_axes,
                                vmem_limit_bytes=_vmem_limit(vmem_bytes))


def _resident(shape, layer=None):
    if layer is None:
        return pl.BlockSpec(shape, lambda *_: (0,) * len(shape), pipeline_mode=pl.Buffered(1))
    return pl.BlockSpec((None,) + tuple(shape[1:]), lambda *_: (layer,) + (0,) * (len(shape) - 1),
                        pipeline_mode=pl.Buffered(1))


def _rms(x, gain):
    return x * lax.rsqrt(jnp.mean(x * x, axis=-1, keepdims=True) + RMS_EPS) * gain


def _dot(a, b):
    return jnp.dot(a, b, preferred_element_type=F32)


def _dot_nt(a, b):
    return lax.dot_general(a, b, (((1,), (1,)), ((), ())), preferred_element_type=F32)


def _qk_lane_layout():
    half = ROPE_DIMS // 2
    lane = np.arange(LANES)
    upper = lane >= ATT_HEAD_DIM
    local = lane - ATT_HEAD_DIM * upper
    is_b = np.where(upper, local >= half, (local >= half) & (local < ROPE_DIMS))
    dim = np.where(local < ROPE_DIMS, local % half + half * upper, local)
    dim = np.where(upper & (local >= ROPE_DIMS), local, dim)
    return is_b.astype(np.int32), dim.astype(np.int32)


def _qk_slab_columns(w):
    is_b, dim = _qk_lane_layout()
    heads = w.reshape(*w.shape[:-1], -1, 2, ATT_HEAD_DIM)
    starts = [0] + [i for i in range(1, LANES) if is_b[i] != is_b[i - 1] or dim[i] != dim[i - 1] + 1] + [LANES]
    runs = [heads[..., int(is_b[lo]), int(dim[lo]):int(dim[lo]) + hi - lo] for lo, hi in zip(starts[:-1], starts[1:])]
    return jnp.concatenate(runs, axis=-1).reshape(w.shape)


def _rope_tables_kernel(rfreq_ref, afreq_ref, asign_ref, rcos_ref, rsin_ref, *att_refs):
    seq = rcos_ref.shape[0]
    row = lax.broadcasted_iota(jnp.int32, (seq, LANES), 0)
    rang = row.astype(F32) * rfreq_ref[...]
    rcos_ref[...] = jnp.cos(rang)
    rsin_ref[...] = jnp.sin(rang)
    tm = TOKEN_TILE
    for i, dil in enumerate(DILATIONS):
        per = tm // dil
        in_tile = row & (tm - 1)
        pos = (row - in_tile) + (in_tile & (per - 1)) * dil + lax.shift_right_logical(in_tile, per.bit_length() - 1)
        ang = pos.astype(F32) * afreq_ref[...]
        att_refs[2 * i][...] = jnp.cos(ang)
        att_refs[2 * i + 1][...] = jnp.sin(ang) * asign_ref[...]


def _rope_tables(seq):
    rfreq = 1.0 / (RET_ROT_BASE ** jnp.linspace(0.0, 1.0, RET_QK_DIM // 2, dtype=F32))
    inv = ROPE_THETA ** (-jnp.arange(0, ROPE_DIMS, 2, dtype=F32) / ROPE_DIMS)
    half = ROPE_DIMS // 2
    _, dim = _qk_lane_layout()
    afreq = jnp.where(dim < ROPE_DIMS, inv[dim % half], 0.0).astype(F32)
    asign = jnp.asarray(np.where(dim < half, -1.0, np.where(dim < ROPE_DIMS, 1.0, 0.0)), F32)
    tab = jax.ShapeDtypeStruct((seq, LANES), F32)
    tabs = pl.pallas_call(
        _rope_tables_kernel,
        out_shape=(tab,) * (2 + 2 * N_GROUPS),
        name="rope_tables",
    )(rfreq[None, :], afreq[None, :], asign[None, :])
    return tabs[0], tabs[1], tabs[2:]


def _partial_rotary(t, cos, signed_sin):
    return t * cos + pltpu.roll(t, ATT_HEAD_DIM, 1) * signed_sin


def _ret_mixer_kernel(lg_ref, x_ref, gain_ref, wq_ref, wk_ref, wv_ref, wg_ref, cos_ref, sin_ref, gn_ref, y_ref,
                      state_ref):
    head = pl.program_id(1)
    lg = lg_ref[head]
    seq = x_ref.shape[0]
    pt, ck = RET_PROJ_ROWS, RET_CHUNK
    half = RET_QK_DIM // 2

    n1 = (lax.broadcasted_iota(jnp.int32, (pt, 1), 0) % ck + 1).astype(F32)
    q_scale = jnp.exp(n1 * lg)
    k_scale = jnp.exp(-n1 * lg) * RET_QK_DIM ** -0.5
    causal = lax.broadcasted_iota(jnp.int32, (ck, ck), 0) >= lax.broadcasted_iota(jnp.int32, (ck, ck), 1)
    chunk_decay = jnp.exp(jnp.full((1, RET_V_DIM), ck, F32) * lg)
    gn = gn_ref[...]

    def project(t):
        rows = slice(t * pt, (t + 1) * pt)
        hb = _rms(x_ref[rows, :], gain_ref[...]).astype(BF16)
        cos, sin = cos_ref[rows, :], sin_ref[rows, :]

        def rotated(acc, scale):
            a, b = acc[:, :half], acc[:, half:]
            return (a * cos - b * sin) * scale, (b * cos + a * sin) * scale

        q_lo, q_hi = rotated(_dot(hb, wq_ref[...]), q_scale)
        k_lo, k_hi = rotated(_dot(hb, wk_ref[...]), k_scale)
        q = jnp.concatenate([q_lo, q_hi], axis=1).astype(BF16)
        kt = jnp.concatenate([k_lo.T, k_hi.T], axis=0).astype(BF16)
        v = _dot(hb, wv_ref[...]).astype(BF16)
        gate = _dot(hb, wg_ref[...])
        g = (gate / (1.0 + jnp.exp(-gate)) * gn).astype(BF16)
        return q, kt, v, g

    def state_free(c, q, kt, v, g, j):
        rows = slice(j * ck, (j + 1) * ck)
        qc, ktc, vc = q[rows], kt[:, rows], v[rows]
        scores = jnp.where(causal, _dot(qc, ktc), 0.0)
        intra = _dot(scores.astype(BF16), vc)
        upd = _dot(ktc, vc) if (c + 1) * ck < seq else None
        return c, qc, intra, upd, g[rows]

    def finish(c, qc, intra, upd, gc):
        y = intra + _dot(qc, state_ref[...].astype(BF16))
        if upd is not None:
            state_ref[...] = (state_ref[...] + upd) * chunk_decay
        mu = jnp.mean(y, axis=-1, keepdims=True)
        yc = y - mu
        var = jnp.mean(yc * yc, axis=-1, keepdims=True)
        yn = (yc * lax.rsqrt(var + GN_EPS)).astype(BF16)
        y_ref[c * ck:(c + 1) * ck, :] = yn * gc

    state_ref[...] = jnp.zeros_like(state_ref)
    n_tiles = seq // pt
    ahead, pending = project(0), []
    for t in range(n_tiles):
        current, ahead = ahead, (project(t + 1) if t + 1 < n_tiles else None)
        for j in range(pt // ck):
            pending.append(state_free(t * (pt // ck) + j, *current, j))
            if len(pending) > RET_SKEW:
                finish(*pending.pop(0))
    for item in pending:
        finish(*item)


def _ret_mixer(x, gain, w, layer, cos, sin, gn_gain, log_gamma, batch, seq):
    tokens, d = x.shape
    dqk = RET_HEADS * RET_QK_DIM
    dv = RET_HEADS * RET_V_DIM
    assert seq % RET_PROJ_ROWS == 0 and RET_PROJ_ROWS % RET_CHUNK == 0
    qk_blocks = dqk // RET_QK_DIM
    v_blocks = dv // RET_V_DIM
    wq = pl.BlockSpec((None, d, RET_QK_DIM), lambda b, h: (layer, 0, h))
    wk = pl.BlockSpec((None, d, RET_QK_DIM), lambda b, h: (layer, 0, qk_blocks + h))
    wv = pl.BlockSpec((None, d, RET_V_DIM), lambda b, h: (layer, 0, 2 * dqk // RET_V_DIM + h))
    wg = pl.BlockSpec((None, d, RET_V_DIM), lambda b, h: (layer, 0, 2 * dqk // RET_V_DIM + v_blocks + h))
    head_cols = pl.BlockSpec((seq, RET_V_DIM), lambda b, h: (b, h))
    vmem = (2 * seq * d * 4 + 2 * d * (2 * RET_QK_DIM + 2 * RET_V_DIM) * 2 + 2 * seq * RET_V_DIM * 2
            + 4 * seq * LANES * 4 + 16 * RET_PROJ_ROWS * RET_V_DIM * 4 + (8 << 20))
    return pl.pallas_call(
        _ret_mixer_kernel,
        grid=(batch, RET_HEADS),
        in_specs=[pl.BlockSpec(memory_space=pltpu.SMEM), pl.BlockSpec((seq, d), lambda b, h: (b, 0)),
                  _resident((1, d)), wq, wk, wv, wg, _resident((seq, LANES)), _resident((seq, LANES)),
                  pl.BlockSpec((1, RET_V_DIM), lambda b, h: (0, h))],
        out_specs=head_cols,
        out_shape=jax.ShapeDtypeStruct((tokens, dv), BF16),
        scratch_shapes=[pltpu.VMEM((RET_QK_DIM, RET_V_DIM), F32)],
        compiler_params=_params(2, vmem),
        name="ret_mixer",
    )(log_gamma, x, gain, w, w, w, w, cos, sin, gn_gain)


def _outproj_kernel(x_ref, a_ref, w_ref, gain_ref, o_ref):
    m = _dot(a_ref[...], w_ref[...])
    o_ref[...] = x_ref[...] + _rms(m, gain_ref[...])


def _outproj(x, a, w, layer, gain):
    tokens, d = x.shape
    kdim = a.shape[1]
    tm = WIDE_TILE
    row = lambda i: (i, 0)
    vmem = 4 * tm * d * 4 + 2 * tm * kdim * 2 + kdim * d * 2 + 4 * tm * d * 4 + (8 << 20)
    return pl.pallas_call(
        _outproj_kernel,
        grid=(tokens // tm,),
        in_specs=[pl.BlockSpec((tm, d), row), pl.BlockSpec((tm, kdim), row), _resident(w.shape, layer),
                  _resident((1, d))],
        out_specs=pl.BlockSpec((tm, d), row),
        out_shape=jax.ShapeDtypeStruct((tokens, d), F32),
        compiler_params=_params(1, vmem),
        name="outproj",
    )(x, a, w, gain)


def _natural_rows(src_ref, row0, rows, lane0, stage_ref, slab, dilation):
    per, first = rows // dilation, row0 // dilation
    if dilation == 1:
        return src_ref[0, first:first + per, lane0:lane0 + LANES].astype(F32)
    for r in range(dilation):
        piece = src_ref[r, first:first + per, lane0:lane0 + LANES]
        stage_ref[slab, pl.ds(r, per, stride=dilation), :] = piece.astype(F32)
    return stage_ref[slab]


def _att_merge_kernel(x_ref, o0_ref, o1_ref, o2_ref, m0_ref, m1_ref, m2_ref, d0_ref, d1_ref, d2_ref,
                      expand_ref, w_ref, gain_ref, out_ref, stage_ref, merged_ref):
    o_refs = (o0_ref, o1_ref, o2_ref)
    tm = x_ref.shape[0]
    sub = min(MERGE_SUB, tm)
    n_sub = tm // sub
    n_slabs = ATT_GROUP_WIDTH // LANES
    per_sub = stage_ref.shape[0] // n_sub

    def weights(s):
        nat = lambda ref, k, g: _natural_rows(ref, s * sub, sub, 0, stage_ref, s * per_sub + k, DILATIONS[g])
        ms = [nat(r, g, g) for g, r in enumerate((m0_ref, m1_ref, m2_ref))]
        dens = [nat(r, N_GROUPS + g, g) for g, r in enumerate((d0_ref, d1_ref, d2_ref))]
        top = jnp.maximum(jnp.maximum(ms[0], ms[1]), ms[2])
        es = [jnp.exp2(m - top) for m in ms]
        inv = 1.0 / (es[0] * dens[0] + es[1] * dens[1] + es[2] * dens[2])
        lane = lax.broadcasted_iota(jnp.int32, top.shape, 1)
        packed = None
        for g, e in enumerate(es):
            alpha = jnp.where(lane < ATT_HEADS, e * inv, 0.0)
            hi = alpha.astype(BF16).astype(F32)
            lo = alpha - hi
            for part, piece in enumerate((hi, lo)):
                shift = (2 * g + part) * ATT_HEADS
                piece = pltpu.roll(piece, shift, 1) if shift else piece
                packed = piece if packed is None else packed + piece
        return packed.astype(BF16)

    def finish(s, packed):
        rows = slice(s * sub, (s + 1) * sub)
        for j in range(n_slabs):
            acc = None
            wide = _dot(packed, expand_ref[:, j * N_GROUPS * LANES:(j + 1) * N_GROUPS * LANES])
            for g in range(N_GROUPS):
                slab = s * per_sub + 2 * N_GROUPS + g * n_slabs + j
                o = _natural_rows(o_refs[g], s * sub, sub, j * LANES, stage_ref, slab, DILATIONS[g])
                term = wide[:, g * LANES:(g + 1) * LANES] * o
                acc = term if acc is None else acc + term
            merged_ref[rows, j * LANES:(j + 1) * LANES] = acc.astype(BF16)
        m = _dot(merged_ref[rows, :], w_ref[...])
        out_ref[rows, :] = x_ref[rows, :] + _rms(m, gain_ref[...])

    ahead = weights(0)
    for s in range(n_sub):
        current, ahead = ahead, (weights(s + 1) if s + 1 < n_sub else None)
        finish(s, current)


def _att_merge(x, outs, maxes, dens, w, layer, gain, seq):
    tokens, d = x.shape
    tm = WIDE_TILE
    sub = min(MERGE_SUB, tm)
    tps = seq // tm
    row = lambda i: (i, 0)
    gw = ATT_GROUP_WIDTH

    def plane(width, dil):
        return pl.BlockSpec((None, dil, tm // dil, width), lambda i: (i // tps, 0, i % tps, 0))

    col = jnp.arange(N_GROUPS * gw)
    slab_j, group_g, lane_l = col // (N_GROUPS * LANES), (col // LANES) % N_GROUPS, col % LANES
    src = 2 * ATT_HEADS * group_g + (slab_j * LANES + lane_l) // ATT_HEAD_DIM
    lane = jnp.arange(LANES)[:, None]
    expand = ((lane == src[None, :]) | (lane == src[None, :] + ATT_HEADS)).astype(BF16)
    n_stage = 2 * N_GROUPS + N_GROUPS * (gw // LANES)
    vmem = (4 * tm * d * 4 + 6 * tm * gw * 2 + 12 * tm * LANES * 4 + gw * d * 2 + expand.size * 2
            + n_stage * tm * LANES * 4 + 8 * tm * d * 4 + (8 << 20))
    return pl.pallas_call(
        _att_merge_kernel,
        grid=(tokens // tm,),
        in_specs=[pl.BlockSpec((tm, d), row)] + [plane(gw, dil) for dil in DILATIONS]
                 + [plane(LANES, dil) for dil in DILATIONS] * 2
                 + [_resident(expand.shape), _resident(w.shape, layer), _resident((1, d))],
        out_specs=pl.BlockSpec((tm, d), row),
        out_shape=jax.ShapeDtypeStruct((tokens, d), F32),
        scratch_shapes=[pltpu.VMEM((n_stage * (tm // sub), sub, LANES), F32), pltpu.VMEM((tm, gw), BF16)],
        compiler_params=_params(1, vmem),
        name="att_merge",
    )(x, *outs, *maxes, *dens, expand, w, gain)


def _ffn_kernel(x_ref, halo_ref, gpre_ref, wup_ref, cw_ref, cb_ref, wdown_ref, gpost_ref, o_ref, *, tiles_per_seq):
    tm = x_ref.shape[0]
    sub = min(FFN_SUB, tm)
    n_sub = tm // sub
    d_ff = wdown_ref.shape[0]
    gpre = gpre_ref[...]
    seq_start = (pl.program_id(0) % tiles_per_seq) == 0
    ext_row = lax.broadcasted_iota(jnp.int32, (CONV_HALO + sub, 1), 0)
    keep = jnp.logical_or(ext_row >= CONV_HALO, jnp.logical_not(seq_start))
    cw, cb = cw_ref[...], cb_ref[...]

    def up(s):
        h = _rms(x_ref[s * sub:(s + 1) * sub, :], gpre)
        if s == 0:
            before = _rms(halo_ref[...], gpre)
        else:
            before = _rms(x_ref[s * sub - CONV_HALO:s * sub, :], gpre)
        hb_ext = jnp.concatenate([before, h], axis=0).astype(BF16)
        gate = _dot(hb_ext, wup_ref[:, :d_ff])
        if s == 0:
            gate = jnp.where(keep, gate, 0.0)
        return gate, _dot(hb_ext[CONV_HALO:], wup_ref[:, d_ff:])

    def mix(gate, val):
        conv = cb
        for tap in reversed(range(CONV_WIDTH)):
            first = CONV_HALO - (CONV_WIDTH - 1 - tap)
            conv = conv + cw[tap:tap + 1] * gate[first:first + sub]
        gelu = 0.5 * conv * (1.0 + jnp.tanh(math.sqrt(2.0 / math.pi) * (conv + 0.044715 * (conv * conv * conv))))
        return _dot((gelu * val).astype(BF16), wdown_ref[...])

    def finish(s, acc):
        rows = slice(s * sub, (s + 1) * sub)
        o_ref[rows, :] = x_ref[rows, :] + _rms(acc, gpost_ref[...])

    ahead, behind = up(0), None
    for s in range(n_sub):
        current, ahead = ahead, (up(s + 1) if s + 1 < n_sub else None)
        acc = mix(*current)
        if behind is not None:
            finish(*behind)
        behind = (s, acc)
    finish(*behind)


def _ffn(x, gpre, wup, cw, cb, wdown, layer, gpost, seq):
    tokens, d = x.shape
    d_ff = wdown.shape[1]
    tm = WIDE_TILE
    tps = seq // tm
    halo_blocks = tm // CONV_HALO
    row = lambda i: (i, 0)
    halo_spec = pl.BlockSpec((CONV_HALO, d), lambda i: (jnp.maximum(i * halo_blocks - 1, 0), 0))
    vmem = 4 * tm * d * 4 + 3 * d * d_ff * 2 + 6 * tm * d * 4 + 16 * tm * d_ff * 4 + (8 << 20)
    return pl.pallas_call(
        functools.partial(_ffn_kernel, tiles_per_seq=tps),
        grid=(tokens // tm,),
        in_specs=[pl.BlockSpec((tm, d), row), halo_spec, _resident((1, d)), _resident(wup.shape, layer),
                  _resident(cw.shape), _resident((1, d_ff)), _resident(wdown.shape, layer), _resident((1, d))],
        out_specs=pl.BlockSpec((tm, d), row),
        out_shape=jax.ShapeDtypeStruct((tokens, d), F32),
        compiler_params=_params(1, vmem),
        name="conv_ffn",
    )(x, x, gpre, wup, cw, cb, wdown, gpost)


def _att_proj_kernel(x_ref, gain_ref, w_ref, *refs, dilations, n_rope):
    n_tab = 2 * N_GROUPS
    tables, outs = refs[:n_tab], refs[n_tab:n_tab + len(dilations)]
    stage_ref, perm_ref = refs[n_tab + len(dilations):]
    tm, d = x_ref.shape
    h = _rms(x_ref[...], gain_ref[...])
    lhs = {1: h.astype(BF16)}
    for j in range(d // LANES):
        stage_ref[j] = h[:, j * LANES:(j + 1) * LANES]
    for pi, dil in enumerate(sorted(set(dilations) - {1})):
        per = tm // dil
        for j in range(d // LANES):
            for r in range(dil):
                piece = stage_ref[j, pl.ds(r, per, stride=dil), :]
                perm_ref[pi, r * per:(r + 1) * per, j * LANES:(j + 1) * LANES] = piece.astype(BF16)
        lhs[dil] = perm_ref.at[pi]
    work = [(s, c0) for s in sorted(range(len(outs)), key=lambda i: dilations[i])
            for c0 in range(0, ATT_GROUP_WIDTH, COL_CHUNK)]

    def project(s, c0):
        dil = dilations[s]
        col = s * ATT_GROUP_WIDTH + c0
        return _dot(lhs[dil] if dil == 1 else lhs[dil][...], w_ref[:, col:col + COL_CHUNK])

    ahead = project(*work[0])
    for i, (s, c0) in enumerate(work):
        acc, ahead = ahead, (project(*work[i + 1]) if i + 1 < len(work) else None)
        o_ref, dil = outs[s], dilations[s]
        gi = DILATIONS.index(dil)
        cos, signed_sin = tables[2 * gi][...], tables[2 * gi + 1][...]
        per = tm // dil
        for j in range(COL_CHUNK // LANES):
            t = acc[:, j * LANES:(j + 1) * LANES]
            if s < n_rope:
                t = _partial_rotary(t, cos, signed_sin)
            t = t.astype(BF16)
            lane0 = c0 + j * LANES
            for r in range(dil):
                o_ref[r, :, lane0:lane0 + LANES] = t[r * per:(r + 1) * per]


def _att_proj(x, gain, w, layer, tables, n_rope, batch, seq):
    tokens, d = x.shape
    n = w.shape[-1]
    gw = ATT_GROUP_WIDTH
    dilations = DILATIONS * (n // (N_GROUPS * gw))
    n_perm = len(set(dilations) - {1})
    tm = TOKEN_TILE
    tps = seq // tm
    row = lambda i: (i, 0)
    tab = pl.BlockSpec((tm, LANES), lambda i: (i % tps, 0))
    plane = lambda dil: pl.BlockSpec((None, dil, tm // dil, gw), lambda i: (i // tps, 0, i % tps, 0))
    vmem = (2 * tm * d * 4 + d * n * 2 + 2 * tm * n * 2 + 8 * tm * COL_CHUNK * 4 + 2 * len(tables) * tm * LANES * 4
            + tm * d * 4 + n_perm * tm * d * 2 + (8 << 20))
    return pl.pallas_call(
        functools.partial(_att_proj_kernel, dilations=dilations, n_rope=n_rope),
        grid=(tokens // tm,),
        in_specs=[pl.BlockSpec((tm, d), row), _resident((1, d)), _resident(w.shape, layer)] + [tab] * len(tables),
        out_specs=[plane(dil) for dil in dilations],
        out_shape=[jax.ShapeDtypeStruct((batch, dil, seq // dil, gw), BF16) for dil in dilations],
        scratch_shapes=[pltpu.VMEM((d // LANES, tm, LANES), F32), pltpu.VMEM((n_perm, tm, d), BF16)],
        compiler_params=_params(1, vmem),
        name="att_proj",
    )(x, gain, w, *tables)


def _dil_attn_kernel(q_ref, k_ref, v_ref, o_ref, m_ref, d_ref, bias_first_ref, bias_band_ref):
    n_planes, length = q_ref.shape[0], q_ref.shape[1]
    blk = ATT_BLOCK
    n_blocks = length // blk
    n_pairs = q_ref.shape[2] // LANES
    lane = lax.broadcasted_iota(jnp.int32, (blk, LANES), 1)
    low = lane < ATT_HEAD_DIM
    is_b, _ = _qk_lane_layout()
    edges = [0] + [i for i in range(1, LANES) if is_b[i] != is_b[i - 1]] + [LANES]
    q_first = None
    for lo, hi in zip(edges[:-1], edges[1:]):
        if not is_b[lo]:
            run = jnp.logical_and(lane >= lo, lane < hi)
            q_first = run if q_first is None else jnp.logical_or(q_first, run)
    zero = jnp.zeros((blk, LANES), BF16)

    def bias(n_keys):
        qi = lax.broadcasted_iota(jnp.int32, (2 * blk, n_keys), 0) % blk
        kj = lax.broadcasted_iota(jnp.int32, (2 * blk, n_keys), 1)
        dist = qi + (n_keys - blk) - kj
        return jnp.where(jnp.logical_and(dist >= 0, dist <= blk), 0.0, NEG_BIG).astype(F32)

    bias_first_ref[...] = bias(blk)
    bias_band_ref[...] = bias(2 * blk)

    def scores(plane, q_rows, k_rows, bias_ref, hp):
        cols = slice(hp * LANES, (hp + 1) * LANES)
        q2 = q_ref[plane, q_rows, cols]
        qq = jnp.concatenate([jnp.where(q_first, q2, zero), jnp.where(q_first, zero, q2)], axis=0)
        return _dot_nt(qq, k_ref[plane, k_rows, cols]) + bias_ref[...]

    def softmax(s):
        m = jnp.max(s, axis=-1, keepdims=True)
        return m, jnp.exp2(s - m).astype(BF16)

    def finish(plane, q_rows, k_rows, bias_ref, hp, mp):
        m, p = mp
        cols = slice(hp * LANES, (hp + 1) * LANES)
        ones = jnp.ones((bias_ref.shape[1], LANES), BF16)
        pv = _dot(p, jnp.concatenate([v_ref[plane, k_rows, cols], ones], axis=1))
        o_ref[plane, q_rows, cols] = jnp.where(low, pv[:blk, :LANES], pv[blk:, :LANES]).astype(BF16)
        for half in range(2):
            head = 2 * hp + half
            rows = slice(half * blk, (half + 1) * blk)
            m_ref[plane, q_rows, head:head + 1] = m[rows]
            d_ref[plane, q_rows, head:head + 1] = pv[rows, LANES + head:LANES + head + 1]

    def run(items):
        work = [(*item, hp) for item in items for hp in range(n_pairs)]
        for plane, q_rows, _, _ in items:
            m_ref[plane, q_rows, :] = jnp.zeros((blk, LANES), F32)
            d_ref[plane, q_rows, :] = jnp.ones((blk, LANES), F32)
        n = len(work)
        s_vals, mp_vals = {}, {}
        for t in range(-ATT_SKEW, n):
            if t + ATT_SKEW < n:
                s_vals[t + ATT_SKEW] = scores(*work[t + ATT_SKEW])
            mid = t + ATT_SKEW // 2
            if 0 <= mid < n and ATT_SKEW > 1:
                mp_vals[mid] = softmax(s_vals.pop(mid))
            if t >= 0:
                if ATT_SKEW <= 1:
                    mp_vals[t] = softmax(s_vals.pop(t))
                finish(*work[t], mp_vals.pop(t))

    run([(plane, pl.ds(0, blk), pl.ds(0, blk), bias_first_ref) for plane in range(n_planes)])
    n_band = n_blocks - 1
    unroll = max(u for u in range(1, ATT_UNROLL + 1) if n_band % u == 0) if n_band else 0
    for plane in range(n_planes if n_band else 0):
        def later_blocks(i, carry, plane=plane):
            items = []
            for j in range(unroll):
                start = (i * unroll + j + 1) * blk
                start = start if isinstance(start, int) else pl.multiple_of(start, blk)
                items.append((plane, pl.ds(start, blk), pl.ds(start - blk, 2 * blk), bias_band_ref))
            run(items)
            return carry

        if n_band == unroll:
            later_blocks(0, 0)
        else:
            lax.fori_loop(0, n_band // unroll, later_blocks, 0)


def _dil_attn(q, k, v):
    batch, dil, length, gw = q.shape
    planes = min(dil, max(1, ATT_MIN_BLOCKS * ATT_BLOCK // length))
    spec = lambda width: pl.BlockSpec((None, planes, length, width), lambda b, r: (b, r, 0, 0))
    blk = ATT_BLOCK
    vmem = (2 * 4 * planes * length * gw * 2 + 4 * planes * length * LANES * 4 + 3 * 2 * blk * 2 * blk * 4
            + 48 * blk * 2 * blk * 4 + (8 << 20))
    stat = jax.ShapeDtypeStruct((batch, dil, length, LANES), F32)
    return pl.pallas_call(
        _dil_attn_kernel,
        grid=(batch, dil // planes),
        in_specs=[spec(gw), spec(gw), spec(gw)],
        out_specs=[spec(gw), spec(LANES), spec(LANES)],
        out_shape=[jax.ShapeDtypeStruct((batch, dil, length, gw), BF16), stat, stat],
        scratch_shapes=[pltpu.VMEM((2 * blk, blk), F32), pltpu.VMEM((2 * blk, 2 * blk), F32)],
        compiler_params=_params(2, vmem),
        name=f"dil_attn_d{dil}",
    )(q, k, v)


def kernel(x, ret_w_in, ret_w_out, ret_gn_gain, kv_norm, att_w_kv, att_w_q, att_w_o, norm_mix_pre, norm_mix_post, norm_ffn_pre, norm_ffn_post, ffn_w_up, ffn_conv_w, ffn_conv_b, ffn_w_down):
    batch, seq, d = x.shape
    depth = norm_mix_pre.shape[0]
    n_ret = ret_w_in.shape[0]
    assert seq % TOKEN_TILE == 0 and seq % WIDE_TILE == 0 and seq % RET_CHUNK == 0
    assert all(seq % (dil * ATT_BLOCK) == 0 and TOKEN_TILE % (16 * dil) == 0 for dil in DILATIONS)
    assert all(win // dil == ATT_BLOCK for win, dil in DIL_GROUPS)

    rcos, rsin, att_tables = _rope_tables(seq)
    q_scale = ATT_HEAD_DIM ** -0.5 * math.log2(math.e)
    n_qk = N_GROUPS * ATT_GROUP_WIDTH
    log_gamma = jnp.log(1.0 - 2.0 ** (-5.0 - jnp.arange(RET_HEADS, dtype=F32)))
    gain = lambda g: g[None, :]
    bf = lambda w: w.astype(BF16)
    w_in, w_out, w_o = bf(ret_w_in), bf(ret_w_out), bf(att_w_o)
    w_q = bf(_qk_slab_columns(att_w_q) * q_scale)
    w_kv = bf(jnp.concatenate([_qk_slab_columns(att_w_kv[:, :n_qk]), att_w_kv[:, n_qk:]], axis=1))
    w_up, w_down = bf(ffn_w_up), bf(ffn_w_down)

    xt = x.reshape(batch * seq, d)
    keys = values = None
    for layer in range(depth):
        if layer < n_ret:
            y = _ret_mixer(xt, gain(norm_mix_pre[layer]), w_in, layer, rcos, rsin,
                           gain(ret_gn_gain[layer]), log_gamma, batch, seq)
            xt = _outproj(xt, y, w_out, layer, gain(norm_mix_post[layer]))
        else:
            bi = layer - n_ret
            queries = _att_proj(xt, gain(norm_mix_pre[layer]), w_q, bi, att_tables, N_GROUPS, batch, seq)
            outs, maxes, dens = zip(*[_dil_attn(queries[gi], keys[gi], values[gi]) for gi in range(N_GROUPS)])
            xt = _att_merge(xt, outs, maxes, dens, w_o, bi, gain(norm_mix_post[layer]), seq)
        xt = _ffn(xt, gain(norm_ffn_pre[layer]), w_up, ffn_conv_w[layer], gain(ffn_conv_b[layer]),
                  w_down, layer, gain(norm_ffn_post[layer]), seq)
        if layer == n_ret - 1:
            kv = _att_proj(xt, gain(kv_norm), w_kv, None, att_tables, N_GROUPS, batch, seq)
            keys, values = kv[:N_GROUPS], kv[N_GROUPS:]
    return xt.reshape(batch, seq, d)
```

```python
import functools
import math

import jax
import jax.numpy as jnp
import numpy as np
from jax import lax
from jax.experimental import pallas as pl
from jax.experimental.pallas import tpu as pltpu

F32 = jnp.float32
BF16 = jnp.bfloat16

LANES = 128
V7X_VMEM_BYTES = 64 * 1024 * 1024

RMS_EPS = 1e-6
GN_EPS = 1e-6
RET_HEADS = 4
RET_QK_DIM = 256
RET_V_DIM = 512
RET_ROT_BASE = 10000.0
ATT_HEADS = 16
ATT_HEAD_DIM = 64
ATT_GROUP_WIDTH = ATT_HEADS * ATT_HEAD_DIM
DIL_GROUPS = ((128, 1), (512, 4), (2048, 16))
DILATIONS = tuple(d for _, d in DIL_GROUPS)
N_GROUPS = 3
ATT_BLOCK = 128
ROPE_THETA = 500000.0
ROPE_DIMS = ATT_HEAD_DIM // 4
CONV_WIDTH = 3

TOKEN_TILE = 512
WIDE_TILE = 1024
COL_CHUNK = 512
FFN_SUB = 512
MERGE_SUB = 512
CONV_HALO = 16
RET_CHUNK = 256
RET_SKEW = 0
RET_PROJ_ROWS = 256
ATT_MIN_BLOCKS = 16
ATT_SKEW = 4
ATT_UNROLL = 15
NEG_BIG = -1e30


def _vmem_limit(nbytes):
    return int(min(V7X_VMEM_BYTES - (4 << 20), max(nbytes, 16 << 20)))


def _params(n_axes, vmem_bytes):
    return pltpu.CompilerParams(dimension_semantics=("arbitrary",) * n_axes,
                                vmem_limit_bytes=_vmem_limit(vmem_bytes))


def _resident(shape, layer=None):
    if layer is None:
        return pl.BlockSpec(shape, lambda *_: (0,) * len(shape), pipeline_mode=pl.Buffered(1))
    return pl.BlockSpec((None,) + tuple(shape[1:]), lambda *_: (layer,) + (0,) * (len(shape) - 1),
                        pipeline_mode=pl.Buffered(1))


def _rms(x, gain):
    return x * lax.rsqrt(jnp.mean(x * x, axis=-1, keepdims=True) + RMS_EPS) * gain


def _dot(a, b):
    return jnp.dot(a, b, preferred_element_type=F32)


def _dot_nt(a, b):
    return lax.dot_general(a, b, (((1,), (1,)), ((), ())), preferred_element_type=F32)


def _qk_lane_layout():
    half = ROPE_DIMS // 2
    lane = np.arange(LANES)
    upper = lane >= ATT_HEAD_DIM
    local = lane - ATT_HEAD_DIM * upper
    is_b = np.where(upper, local >= half, (local >= half) & (local < ROPE_DIMS))
    dim = np.where(local < ROPE_DIMS, local % half + half * upper, local)
    dim = np.where(upper & (local >= ROPE_DIMS), local, dim)
    return is_b.astype(np.int32), dim.astype(np.int32)


def _qk_slab_columns(w):
    is_b, dim = _qk_lane_layout()
    heads = w.reshape(*w.shape[:-1], -1, 2, ATT_HEAD_DIM)
    starts = [0] + [i for i in range(1, LANES) if is_b[i] != is_b[i - 1] or dim[i] != dim[i - 1] + 1] + [LANES]
    runs = [heads[..., int(is_b[lo]), int(dim[lo]):int(dim[lo]) + hi - lo] for lo, hi in zip(starts[:-1], starts[1:])]
    return jnp.concatenate(runs, axis=-1).reshape(w.shape)


def _rope_tables_kernel(rfreq_ref, afreq_ref, asign_ref, rcos_ref, rsin_ref, *att_refs):
    seq = rcos_ref.shape[0]
    row = lax.broadcasted_iota(jnp.int32, (seq, LANES), 0)
    rang = row.astype(F32) * rfreq_ref[...]
    rcos_ref[...] = jnp.cos(rang)
    rsin_ref[...] = jnp.sin(rang)
    tm = TOKEN_TILE
    for i, dil in enumerate(DILATIONS):
        per = tm // dil
        in_tile = row & (tm - 1)
        pos = (row - in_tile) + (in_tile & (per - 1)) * dil + lax.shift_right_logical(in_tile, per.bit_length() - 1)
        ang = pos.astype(F32) * afreq_ref[...]
        att_refs[2 * i][...] = jnp.cos(ang)
        att_refs[2 * i + 1][...] = jnp.sin(ang) * asign_ref[...]


def _rope_tables(seq):
    rfreq = 1.0 / (RET_ROT_BASE ** jnp.linspace(0.0, 1.0, RET_QK_DIM // 2, dtype=F32))
    inv = ROPE_THETA ** (-jnp.arange(0, ROPE_DIMS, 2, dtype=F32) / ROPE_DIMS)
    half = ROPE_DIMS // 2
    _, dim = _qk_lane_layout()
    afreq = jnp.where(dim < ROPE_DIMS, inv[dim % half], 0.0).astype(F32)
    asign = jnp.asarray(np.where(dim < half, -1.0, np.where(dim < ROPE_DIMS, 1.0, 0.0)), F32)
    tab = jax.ShapeDtypeStruct((seq, LANES), F32)
    tabs = pl.pallas_call(
        _rope_tables_kernel,
        out_shape=(tab,) * (2 + 2 * N_GROUPS),
        name="rope_tables",
    )(rfreq[None, :], afreq[None, :], asign[None, :])
    return tabs[0], tabs[1], tabs[2:]


def _partial_rotary(t, cos, signed_sin):
    return t * cos + pltpu.roll(t, ATT_HEAD_DIM, 1) * signed_sin


def _ret_mixer_kernel(lg_ref, x_ref, gain_ref, wq_ref, wk_ref, wv_ref, wg_ref, cos_ref, sin_ref, gn_ref, y_ref,
                      state_ref):
    head = pl.program_id(1)
    lg = lg_ref[head]
    seq = x_ref.shape[0]
    pt, ck = RET_PROJ_ROWS, RET_CHUNK
    half = RET_QK_DIM // 2

    n1 = (lax.broadcasted_iota(jnp.int32, (pt, 1), 0) % ck + 1).astype(F32)
    q_scale = jnp.exp(n1 * lg)
    k_scale = jnp.exp(-n1 * lg) * RET_QK_DIM ** -0.5
    causal = lax.broadcasted_iota(jnp.int32, (ck, ck), 0) >= lax.broadcasted_iota(jnp.int32, (ck, ck), 1)
    chunk_decay = jnp.exp(jnp.full((1, RET_V_DIM), ck, F32) * lg)
    gn = gn_ref[...]

    def project(t):
        rows = slice(t * pt, (t + 1) * pt)
        hb = _rms(x_ref[rows, :], gain_ref[...]).astype(BF16)
        cos, sin = cos_ref[rows, :], sin_ref[rows, :]

        def rotated(acc, scale):
            a, b = acc[:, :half], acc[:, half:]
            return (a * cos - b * sin) * scale, (b * cos + a * sin) * scale

        q_lo, q_hi = rotated(_dot(hb, wq_ref[...]), q_scale)
        k_lo, k_hi = rotated(_dot(hb, wk_ref[...]), k_scale)
        q = jnp.concatenate([q_lo, q_hi], axis=1).astype(BF16)
        kt = jnp.concatenate([k_lo.T, k_hi.T], axis=0).astype(BF16)
        v = _dot(hb, wv_ref[...]).astype(BF16)
        gate = _dot(hb, wg_ref[...])
        g = (gate / (1.0 + jnp.exp(-gate)) * gn).astype(BF16)
        return q, kt, v, g

    def state_free(c, q, kt, v, g, j):
        rows = slice(j * ck, (j + 1) * ck)
        qc, ktc, vc = q[rows], kt[:, rows], v[rows]
        scores = jnp.where(causal, _dot(qc, ktc), 0.0)
        intra = _dot(scores.astype(BF16), vc)
        upd = _dot(ktc, vc) if (c + 1) * ck < seq else None
        return c, qc, intra, upd, g[rows]

    def finish(c, qc, intra, upd, gc):
        y = intra + _dot(qc, state_ref[...].astype(BF16))
        if upd is not None:
            state_ref[...] = (state_ref[...] + upd) * chunk_decay
        mu = jnp.mean(y, axis=-1, keepdims=True)
        yc = y - mu
        var = jnp.mean(yc * yc, axis=-1, keepdims=True)
        yn = (yc * lax.rsqrt(var + GN_EPS)).astype(BF16)
        y_ref[c * ck:(c + 1) * ck, :] = yn * gc

    state_ref[...] = jnp.zeros_like(state_ref)
    n_tiles = seq // pt
    ahead, pending = project(0), []
    for t in range(n_tiles):
        current, ahead = ahead, (project(t + 1) if t + 1 < n_tiles else None)
        for j in range(pt // ck):
            pending.append(state_free(t * (pt // ck) + j, *current, j))
            if len(pending) > RET_SKEW:
                finish(*pending.pop(0))
    for item in pending:
        finish(*item)


def _ret_mixer(x, gain, w, layer, cos, sin, gn_gain, log_gamma, batch, seq):
    tokens, d = x.shape
    dqk = RET_HEADS * RET_QK_DIM
    dv = RET_HEADS * RET_V_DIM
    assert seq % RET_PROJ_ROWS == 0 and RET_PROJ_ROWS % RET_CHUNK == 0
    qk_blocks = dqk // RET_QK_DIM
    v_blocks = dv // RET_V_DIM
    wq = pl.BlockSpec((None, d, RET_QK_DIM), lambda b, h: (layer, 0, h))
    wk = pl.BlockSpec((None, d, RET_QK_DIM), lambda b, h: (layer, 0, qk_blocks + h))
    wv = pl.BlockSpec((None, d, RET_V_DIM), lambda b, h: (layer, 0, 2 * dqk // RET_V_DIM + h))
    wg = pl.BlockSpec((None, d, RET_V_DIM), lambda b, h: (layer, 0, 2 * dqk // RET_V_DIM + v_blocks + h))
    head_cols = pl.BlockSpec((seq, RET_V_DIM), lambda b, h: (b, h))
    vmem = (2 * seq * d * 4 + 2 * d * (2 * RET_QK_DIM + 2 * RET_V_DIM) * 2 + 2 * seq * RET_V_DIM * 2
            + 4 * seq * LANES * 4 + 16 * RET_PROJ_ROWS * RET_V_DIM * 4 + (8 << 20))
    return pl.pallas_call(
        _ret_mixer_kernel,
        grid=(batch, RET_HEADS),
        in_specs=[pl.BlockSpec(memory_space=pltpu.SMEM), pl.BlockSpec((seq, d), lambda b, h: (b, 0)),
                  _resident((1, d)), wq, wk, wv, wg, _resident((seq, LANES)), _resident((seq, LANES)),
                  pl.BlockSpec((1, RET_V_DIM), lambda b, h: (0, h))],
        out_specs=head_cols,
        out_shape=jax.ShapeDtypeStruct((tokens, dv), BF16),
        scratch_shapes=[pltpu.VMEM((RET_QK_DIM, RET_V_DIM), F32)],
        compiler_params=_params(2, vmem),
        name="ret_mixer",
    )(log_gamma, x, gain, w, w, w, w, cos, sin, gn_gain)


def _outproj_kernel(x_ref, a_ref, w_ref, gain_ref, o_ref):
    m = _dot(a_ref[...], w_ref[...])
    o_ref[...] = x_ref[...] + _rms(m, gain_ref[...])


def _outproj(x, a, w, layer, gain):
    tokens, d = x.shape
    kdim = a.shape[1]
    tm = WIDE_TILE
    row = lambda i: (i, 0)
    vmem = 4 * tm * d * 4 + 2 * tm * kdim * 2 + kdim * d * 2 + 4 * tm * d * 4 + (8 << 20)
    return pl.pallas_call(
        _outproj_kernel,
        grid=(tokens // tm,),
        in_specs=[pl.BlockSpec((tm, d), row), pl.BlockSpec((tm, kdim), row), _resident(w.shape, layer),
                  _resident((1, d))],
        out_specs=pl.BlockSpec((tm, d), row),
        out_shape=jax.ShapeDtypeStruct((tokens, d), F32),
        compiler_params=_params(1, vmem),
        name="outproj",
    )(x, a, w, gain)


def _natural_rows(src_ref, row0, rows, lane0, stage_ref, slab, dilation):
    per, first = rows // dilation, row0 // dilation
    if dilation == 1:
        return src_ref[0, first:first + per, lane0:lane0 + LANES].astype(F32)
    for r in range(dilation):
        piece = src_ref[r, first:first + per, lane0:lane0 + LANES]
        stage_ref[slab, pl.ds(r, per, stride=dilation), :] = piece.astype(F32)
    return stage_ref[slab]


def _att_merge_kernel(x_ref, o0_ref, o1_ref, o2_ref, m0_ref, m1_ref, m2_ref, d0_ref, d1_ref, d2_ref,
                      expand_ref, w_ref, gain_ref, out_ref, stage_ref, merged_ref):
    o_refs = (o0_ref, o1_ref, o2_ref)
    tm = x_ref.shape[0]
    sub = min(MERGE_SUB, tm)
    n_sub = tm // sub
    n_slabs = ATT_GROUP_WIDTH // LANES
    per_sub = stage_ref.shape[0] // n_sub

    def weights(s):
        nat = lambda ref, k, g: _natural_rows(ref, s * sub, sub, 0, stage_ref, s * per_sub + k, DILATIONS[g])
        ms = [nat(r, g, g) for g, r in enumerate((m0_ref, m1_ref, m2_ref))]
        dens = [nat(r, N_GROUPS + g, g) for g, r in enumerate((d0_ref, d1_ref, d2_ref))]
        top = jnp.maximum(jnp.maximum(ms[0], ms[1]), ms[2])
        es = [jnp.exp2(m - top) for m in ms]
        inv = 1.0 / (es[0] * dens[0] + es[1] * dens[1] + es[2] * dens[2])
        lane = lax.broadcasted_iota(jnp.int32, top.shape, 1)
        packed = None
        for g, e in enumerate(es):
            alpha = jnp.where(lane < ATT_HEADS, e * inv, 0.0)
            hi = alpha.astype(BF16).astype(F32)
            lo = alpha - hi
            for part, piece in enumerate((hi, lo)):
                shift = (2 * g + part) * ATT_HEADS
                piece = pltpu.roll(piece, shift, 1) if shift else piece
                packed = piece if packed is None else packed + piece
        return packed.astype(BF16)

    def finish(s, packed):
        rows = slice(s * sub, (s + 1) * sub)
        for j in range(n_slabs):
            acc = None
            wide = _dot(packed, expand_ref[:, j * N_GROUPS * LANES:(j + 1) * N_GROUPS * LANES])
            for g in range(N_GROUPS):
                slab = s * per_sub + 2 * N_GROUPS + g * n_slabs + j
                o = _natural_rows(o_refs[g], s * sub, sub, j * LANES, stage_ref, slab, DILATIONS[g])
                term = wide[:, g * LANES:(g + 1) * LANES] * o
                acc = term if acc is None else acc + term
            merged_ref[rows, j * LANES:(j + 1) * LANES] = acc.astype(BF16)
        m = _dot(merged_ref[rows, :], w_ref[...])
        out_ref[rows, :] = x_ref[rows, :] + _rms(m, gain_ref[...])

    ahead = weights(0)
    for s in range(n_sub):
        current, ahead = ahead, (weights(s + 1) if s + 1 < n_sub else None)
        finish(s, current)


def _att_merge(x, outs, maxes, dens, w, layer, gain, seq):
    tokens, d = x.shape
    tm = WIDE_TILE
    sub = min(MERGE_SUB, tm)
    tps = seq // tm
    row = lambda i: (i, 0)
    gw = ATT_GROUP_WIDTH

    def plane(width, dil):
        return pl.BlockSpec((None, dil, tm // dil, width), lambda i: (i // tps, 0, i % tps, 0))

    col = jnp.arange(N_GROUPS * gw)
    slab_j, group_g, lane_l = col // (N_GROUPS * LANES), (col // LANES) % N_GROUPS, col % LANES
    src = 2 * ATT_HEADS * group_g + (slab_j * LANES + lane_l) // ATT_HEAD_DIM
    lane = jnp.arange(LANES)[:, None]
    expand = ((lane == src[None, :]) | (lane == src[None, :] + ATT_HEADS)).astype(BF16)
    n_stage = 2 * N_GROUPS + N_GROUPS * (gw // LANES)
    vmem = (4 * tm * d * 4 + 6 * tm * gw * 2 + 12 * tm * LANES * 4 + gw * d * 2 + expand.size * 2
            + n_stage * tm * LANES * 4 + 8 * tm * d * 4 + (8 << 20))
    return pl.pallas_call(
        _att_merge_kernel,
        grid=(tokens // tm,),
        in_specs=[pl.BlockSpec((tm, d), row)] + [plane(gw, dil) for dil in DILATIONS]
                 + [plane(LANES, dil) for dil in DILATIONS] * 2
                 + [_resident(expand.shape), _resident(w.shape, layer), _resident((1, d))],
        out_specs=pl.BlockSpec((tm, d), row),
        out_shape=jax.ShapeDtypeStruct((tokens, d), F32),
        scratch_shapes=[pltpu.VMEM((n_stage * (tm // sub), sub, LANES), F32), pltpu.VMEM((tm, gw), BF16)],
        compiler_params=_params(1, vmem),
        name="att_merge",
    )(x, *outs, *maxes, *dens, expand, w, gain)


def _ffn_kernel(x_ref, halo_ref, gpre_ref, wup_ref, cw_ref, cb_ref, wdown_ref, gpost_ref, o_ref, *, tiles_per_seq):
    tm = x_ref.shape[0]
    sub = min(FFN_SUB, tm)
    n_sub = tm // sub
    d_ff = wdown_ref.shape[0]
    gpre = gpre_ref[...]
    seq_start = (pl.program_id(0) % tiles_per_seq) == 0
    ext_row = lax.broadcasted_iota(jnp.int32, (CONV_HALO + sub, 1), 0)
    keep = jnp.logical_or(ext_row >= CONV_HALO, jnp.logical_not(seq_start))
    cw, cb = cw_ref[...], cb_ref[...]

    def up(s):
        h = _rms(x_ref[s * sub:(s + 1) * sub, :], gpre)
        if s == 0:
            before = _rms(halo_ref[...], gpre)
        else:
            before = _rms(x_ref[s * sub - CONV_HALO:s * sub, :], gpre)
        hb_ext = jnp.concatenate([before, h], axis=0).astype(BF16)
        gate = _dot(hb_ext, wup_ref[:, :d_ff])
        if s == 0:
            gate = jnp.where(keep, gate, 0.0)
        return gate, _dot(hb_ext[CONV_HALO:], wup_ref[:, d_ff:])

    def mix(gate, val):
        conv = cb
        for tap in reversed(range(CONV_WIDTH)):
            first = CONV_HALO - (CONV_WIDTH - 1 - tap)
            conv = conv + cw[tap:tap + 1] * gate[first:first + sub]
        gelu = 0.5 * conv * (1.0 + jnp.tanh(math.sqrt(2.0 / math.pi) * (conv + 0.044715 * (conv * conv * conv))))
        return _dot((gelu * val).astype(BF16), wdown_ref[...])

    def finish(s, acc):
        rows = slice(s * sub, (s + 1) * sub)
        o_ref[rows, :] = x_ref[rows, :] + _rms(acc, gpost_ref[...])

    ahead, behind = up(0), None
    for s in range(n_sub):
        current, ahead = ahead, (up(s + 1) if s + 1 < n_sub else None)
        acc = mix(*current)
        if behind is not None:
            finish(*behind)
        behind = (s, acc)
    finish(*behind)


def _ffn(x, gpre, wup, cw, cb, wdown, layer, gpost, seq):
    tokens, d = x.shape
    d_ff = wdown.shape[1]
    tm = WIDE_TILE
    tps = seq // tm
    halo_blocks = tm // CONV_HALO
    row = lambda i: (i, 0)
    halo_spec = pl.BlockSpec((CONV_HALO, d), lambda i: (jnp.maximum(i * halo_blocks - 1, 0), 0))
    vmem = 4 * tm * d * 4 + 3 * d * d_ff * 2 + 6 * tm * d * 4 + 16 * tm * d_ff * 4 + (8 << 20)
    return pl.pallas_call(
        functools.partial(_ffn_kernel, tiles_per_seq=tps),
        grid=(tokens // tm,),
        in_specs=[pl.BlockSpec((tm, d), row), halo_spec, _resident((1, d)), _resident(wup.shape, layer),
                  _resident(cw.shape), _resident((1, d_ff)), _resident(wdown.shape, layer), _resident((1, d))],
        out_specs=pl.BlockSpec((tm, d), row),
        out_shape=jax.ShapeDtypeStruct((tokens, d), F32),
        compiler_params=_params(1, vmem),
        name="conv_ffn",
    )(x, x, gpre, wup, cw, cb, wdown, gpost)


def _att_proj_kernel(x_ref, gain_ref, w_ref, *refs, dilations, n_rope):
    n_tab = 2 * N_GROUPS
    tables, outs = refs[:n_tab], refs[n_tab:n_tab + len(dilations)]
    stage_ref, perm_ref = refs[n_tab + len(dilations):]
    tm, d = x_ref.shape
    h = _rms(x_ref[...], gain_ref[...])
    lhs = {1: h.astype(BF16)}
    for j in range(d // LANES):
        stage_ref[0, j] = h[:, j * LANES:(j + 1) * LANES]
    permuted = sorted(set(dilations) - {1})
    src_dil = 1
    for pi, dil in enumerate(permuted):
        step, per, per_src = dil // src_dil, tm // dil, tm // src_dil
        for j in range(d // LANES):
            for rp in range(src_dil):
                for r1 in range(step):
                    piece = stage_ref[pi, j, pl.ds(rp * per_src + r1, per, stride=step), :]
                    rows = slice((r1 * src_dil + rp) * per, (r1 * src_dil + rp + 1) * per)
                    if pi + 1 < len(permuted):
                        stage_ref[pi + 1, j, rows, :] = piece
                    perm_ref[pi, rows, j * LANES:(j + 1) * LANES] = piece.astype(BF16)
        lhs[dil] = perm_ref.at[pi]
        src_dil = dil
    work = [(s, c0) for s in sorted(range(len(outs)), key=lambda i: dilations[i])
            for c0 in range(0, ATT_GROUP_WIDTH, COL_CHUNK)]

    def project(s, c0):
        dil = dilations[s]
        col = s * ATT_GROUP_WIDTH + c0
        return _dot(lhs[dil] if dil == 1 else lhs[dil][...], w_ref[:, col:col + COL_CHUNK])

    ahead = project(*work[0])
    for i, (s, c0) in enumerate(work):
        acc, ahead = ahead, (project(*work[i + 1]) if i + 1 < len(work) else None)
        o_ref, dil = outs[s], dilations[s]
        gi = DILATIONS.index(dil)
        cos, signed_sin = tables[2 * gi][...], tables[2 * gi + 1][...]
        per = tm // dil
        for j in range(COL_CHUNK // LANES):
            t = acc[:, j * LANES:(j + 1) * LANES]
            if s < n_rope:
                t = _partial_rotary(t, cos, signed_sin)
            t = t.astype(BF16)
            lane0 = c0 + j * LANES
            for r in range(dil):
                o_ref[r, :, lane0:lane0 + LANES] = t[r * per:(r + 1) * per]


def _att_proj(x, gain, w, layer, tables, n_rope, batch, seq):
    tokens, d = x.shape
    n = w.shape[-1]
    gw = ATT_GROUP_WIDTH
    dilations = DILATIONS * (n // (N_GROUPS * gw))
    n_perm = len(set(dilations) - {1})
    tm = TOKEN_TILE
    tps = seq // tm
    row = lambda i: (i, 0)
    tab = pl.BlockSpec((tm, LANES), lambda i: (i % tps, 0))
    plane = lambda dil: pl.BlockSpec((None, dil, tm // dil, gw), lambda i: (i // tps, 0, i % tps, 0))
    vmem = (2 * tm * d * 4 + d * n * 2 + 2 * tm * n * 2 + 8 * tm * COL_CHUNK * 4 + 2 * len(tables) * tm * LANES * 4
            + n_perm * tm * d * (4 + 2) + (8 << 20))
    return pl.pallas_call(
        functools.partial(_att_proj_kernel, dilations=dilations, n_rope=n_rope),
        grid=(tokens // tm,),
        in_specs=[pl.BlockSpec((tm, d), row), _resident((1, d)), _resident(w.shape, layer)] + [tab] * len(tables),
        out_specs=[plane(dil) for dil in dilations],
        out_shape=[jax.ShapeDtypeStruct((batch, dil, seq // dil, gw), BF16) for dil in dilations],
        scratch_shapes=[pltpu.VMEM((n_perm, d // LANES, tm, LANES), F32), pltpu.VMEM((n_perm, tm, d), BF16)],
        compiler_params=_params(1, vmem),
        name="att_proj",
    )(x, gain, w, *tables)


def _dil_attn_kernel(q_ref, k_ref, v_ref, o_ref, m_ref, d_ref, bias_first_ref, bias_band_ref):
    n_planes, length = q_ref.shape[0], q_ref.shape[1]
    blk = ATT_BLOCK
    n_blocks = length // blk
    n_pairs = q_ref.shape[2] // LANES
    lane = lax.broadcasted_iota(jnp.int32, (blk, LANES), 1)
    low = lane < ATT_HEAD_DIM
    is_b, _ = _qk_lane_layout()
    edges = [0] + [i for i in range(1, LANES) if is_b[i] != is_b[i - 1]] + [LANES]
    q_first = None
    for lo, hi in zip(edges[:-1], edges[1:]):
        if not is_b[lo]:
            run = jnp.logical_and(lane >= lo, lane < hi)
            q_first = run if q_first is None else jnp.logical_or(q_first, run)
    zero = jnp.zeros((blk, LANES), BF16)

    def bias(n_keys):
        qi = lax.broadcasted_iota(jnp.int32, (2 * blk, n_keys), 0) % blk
        kj = lax.broadcasted_iota(jnp.int32, (2 * blk, n_keys), 1)
        dist = qi + (n_keys - blk) - kj
        return jnp.where(jnp.logical_and(dist >= 0, dist <= blk), 0.0, NEG_BIG).astype(F32)

    bias_first_ref[...] = bias(blk)
    bias_band_ref[...] = bias(2 * blk)

    def scores(plane, q_rows, k_rows, bias_ref, hp):
        cols = slice(hp * LANES, (hp + 1) * LANES)
        q2 = q_ref[plane, q_rows, cols]
        qq = jnp.concatenate([jnp.where(q_first, q2, zero), jnp.where(q_first, zero, q2)], axis=0)
        return _dot_nt(qq, k_ref[plane, k_rows, cols]) + bias_ref[...]

    def softmax(s):
        m = jnp.max(s, axis=-1, keepdims=True)
        return m, jnp.exp2(s - m).astype(BF16)

    def finish(plane, q_rows, k_rows, bias_ref, hp, mp):
        m, p = mp
        cols = slice(hp * LANES, (hp + 1) * LANES)
        ones = jnp.ones((bias_ref.shape[1], LANES), BF16)
        pv = _dot(p, jnp.concatenate([v_ref[plane, k_rows, cols], ones], axis=1))
        o_ref[plane, q_rows, cols] = jnp.where(low, pv[:blk, :LANES], pv[blk:, :LANES]).astype(BF16)
        for half in range(2):
            head = 2 * hp + half
            rows = slice(half * blk, (half + 1) * blk)
            m_ref[plane, q_rows, head:head + 1] = m[rows]
            d_ref[plane, q_rows, head:head + 1] = pv[rows, LANES + head:LANES + head + 1]

    def run(items):
        work = [(*item, hp) for item in items for hp in range(n_pairs)]
        for plane, q_rows, _, _ in items:
            m_ref[plane, q_rows, :] = jnp.zeros((blk, LANES), F32)
            d_ref[plane, q_rows, :] = jnp.ones((blk, LANES), F32)
        n = len(work)
        s_vals, mp_vals = {}, {}
        for t in range(-ATT_SKEW, n):
            if t + ATT_SKEW < n:
                s_vals[t + ATT_SKEW] = scores(*work[t + ATT_SKEW])
            mid = t + ATT_SKEW // 2
            if 0 <= mid < n and ATT_SKEW > 1:
                mp_vals[mid] = softmax(s_vals.pop(mid))
            if t >= 0:
                if ATT_SKEW <= 1:
                    mp_vals[t] = softmax(s_vals.pop(t))
                finish(*work[t], mp_vals.pop(t))

    run([(plane, pl.ds(0, blk), pl.ds(0, blk), bias_first_ref) for plane in range(n_planes)])
    n_band = n_blocks - 1
    unroll = max(u for u in range(1, ATT_UNROLL + 1) if n_band % u == 0) if n_band else 0
    for plane in range(n_planes if n_band else 0):
        def later_blocks(i, carry, plane=plane):
            items = []
            for j in range(unroll):
                start = (i * unroll + j + 1) * blk
                start = start if isinstance(start, int) else pl.multiple_of(start, blk)
                items.append((plane, pl.ds(start, blk), pl.ds(start - blk, 2 * blk), bias_band_ref))
            run(items)
            return carry

        if n_band == unroll:
            later_blocks(0, 0)
        else:
            lax.fori_loop(0, n_band // unroll, later_blocks, 0)


def _dil_attn(q, k, v):
    batch, dil, length, gw = q.shape
    planes = min(dil, max(1, ATT_MIN_BLOCKS * ATT_BLOCK // length))
    spec = lambda width: pl.BlockSpec((None, planes, length, width), lambda b, r: (b, r, 0, 0))
    blk = ATT_BLOCK
    vmem = (2 * 4 * planes * length * gw * 2 + 4 * planes * length * LANES * 4 + 3 * 2 * blk * 2 * blk * 4
            + 48 * blk * 2 * blk * 4 + (8 << 20))
    stat = jax.ShapeDtypeStruct((batch, dil, length, LANES), F32)
    return pl.pallas_call(
        _dil_attn_kernel,
        grid=(batch, dil // planes),
        in_specs=[spec(gw), spec(gw), spec(gw)],
        out_specs=[spec(gw), spec(LANES), spec(LANES)],
        out_shape=[jax.ShapeDtypeStruct((batch, dil, length, gw), BF16), stat, stat],
        scratch_shapes=[pltpu.VMEM((2 * blk, blk), F32), pltpu.VMEM((2 * blk, 2 * blk), F32)],
        compiler_params=_params(2, vmem),
        name=f"dil_attn_d{dil}",
    )(q, k, v)


def kernel(x, ret_w_in, ret_w_out, ret_gn_gain, kv_norm, att_w_kv, att_w_q, att_w_o, norm_mix_pre, norm_mix_post, norm_ffn_pre, norm_ffn_post, ffn_w_up, ffn_conv_w, ffn_conv_b, ffn_w_down):
    batch, seq, d = x.shape
    depth = norm_mix_pre.shape[0]
    n_ret = ret_w_in.shape[0]
    assert seq % TOKEN_TILE == 0 and seq % WIDE_TILE == 0 and seq % RET_CHUNK == 0
    assert all(seq % (dil * ATT_BLOCK) == 0 and TOKEN_TILE % (16 * dil) == 0 for dil in DILATIONS)
    assert all(win // dil == ATT_BLOCK for win, dil in DIL_GROUPS)

    rcos, rsin, att_tables = _rope_tables(seq)
    q_scale = ATT_HEAD_DIM ** -0.5 * math.log2(math.e)
    n_qk = N_GROUPS * ATT_GROUP_WIDTH
    log_gamma = jnp.log(1.0 - 2.0 ** (-5.0 - jnp.arange(RET_HEADS, dtype=F32)))
    gain = lambda g: g[None, :]
    bf = lambda w: w.astype(BF16)
    w_in, w_out, w_o = bf(ret_w_in), bf(ret_w_out), bf(att_w_o)
    w_q = bf(_qk_slab_columns(att_w_q) * q_scale)
    w_kv = bf(jnp.concatenate([_qk_slab_columns(att_w_kv[:, :n_qk]), att_w_kv[:, n_qk:]], axis=1))
    w_up, w_down = bf(ffn_w_up), bf(ffn_w_down)

    xt = x.reshape(batch * seq, d)
    keys = values = None
    for layer in range(depth):
        if layer < n_ret:
            y = _ret_mixer(xt, gain(norm_mix_pre[layer]), w_in, layer, rcos, rsin,
                           gain(ret_gn_gain[layer]), log_gamma, batch, seq)
            xt = _outproj(xt, y, w_out, layer, gain(norm_mix_post[layer]))
        else:
            bi = layer - n_ret
            queries = _att_proj(xt, gain(norm_mix_pre[layer]), w_q, bi, att_tables, N_GROUPS, batch, seq)
            outs, maxes, dens = zip(*[_dil_attn(queries[gi], keys[gi], values[gi]) for gi in range(N_GROUPS)])
            xt = _att_merge(xt, outs, maxes, dens, w_o, bi, gain(norm_mix_post[layer]), seq)
        xt = _ffn(xt, gain(norm_ffn_pre[layer]), w_up, ffn_conv_w[layer], gain(ffn_conv_b[layer]),
                  w_down, layer, gain(norm_ffn_post[layer]), seq)
        if layer == n_ret - 1:
            kv = _att_proj(xt, gain(kv_norm), w_kv, None, att_tables, N_GROUPS, batch, seq)
            keys, values = kv[:N_GROUPS], kv[N_GROUPS:]
    return xt.reshape(batch, seq, d)
```

```python
import functools
import math

import jax
import jax.numpy as jnp
import numpy as np
from jax import lax
from jax.experimental import pallas as pl
from jax.experimental.pallas import tpu as pltpu

F32 = jnp.float32
BF16 = jnp.bfloat16

LANES = 128
V7X_VMEM_BYTES = 64 * 1024 * 1024

RMS_EPS = 1e-6
GN_EPS = 1e-6
RET_HEADS = 4
RET_QK_DIM = 256
RET_V_DIM = 512
RET_ROT_BASE = 10000.0
ATT_HEADS = 16
ATT_HEAD_DIM = 64
ATT_GROUP_WIDTH = ATT_HEADS * ATT_HEAD_DIM
DIL_GROUPS = ((128, 1), (512, 4), (2048, 16))
DILATIONS = tuple(d for _, d in DIL_GROUPS)
N_GROUPS = 3
ATT_BLOCK = 128
ROPE_THETA = 500000.0
ROPE_DIMS = ATT_HEAD_DIM // 4
CONV_WIDTH = 3

TOKEN_TILE = 512
WIDE_TILE = 1024
COL_CHUNK = 512
FFN_SUB = 512
MERGE_SUB = 512
RELAYOUT_STRIDE = 4
CONV_HALO = 16
RET_CHUNK = 256
RET_SKEW = 0
RET_PROJ_ROWS = 256
ATT_MIN_BLOCKS = 16
ATT_SKEW = 4
ATT_UNROLL = 15
NEG_BIG = -1e30


def _vmem_limit(nbytes):
    return int(min(V7X_VMEM_BYTES - (4 << 20), max(nbytes, 16 << 20)))


def _params(n_axes, vmem_bytes):
    return pltpu.CompilerParams(dimension_semantics=("arbitrary",) * n_axes,
                                vmem_limit_bytes=_vmem_limit(vmem_bytes))


def _resident(shape, layer=None):
    if layer is None:
        return pl.BlockSpec(shape, lambda *_: (0,) * len(shape), pipeline_mode=pl.Buffered(1))
    return pl.BlockSpec((None,) + tuple(shape[1:]), lambda *_: (layer,) + (0,) * (len(shape) - 1),
                        pipeline_mode=pl.Buffered(1))


def _rms(x, gain):
    return x * lax.rsqrt(jnp.mean(x * x, axis=-1, keepdims=True) + RMS_EPS) * gain


def _dot(a, b):
    return jnp.dot(a, b, preferred_element_type=F32)


def _dot_nt(a, b):
    return lax.dot_general(a, b, (((1,), (1,)), ((), ())), preferred_element_type=F32)


def _qk_lane_layout():
    half = ROPE_DIMS // 2
    lane = np.arange(LANES)
    upper = lane >= ATT_HEAD_DIM
    local = lane - ATT_HEAD_DIM * upper
    is_b = np.where(upper, local >= half, (local >= half) & (local < ROPE_DIMS))
    dim = np.where(local < ROPE_DIMS, local % half + half * upper, local)
    dim = np.where(upper & (local >= ROPE_DIMS), local, dim)
    return is_b.astype(np.int32), dim.astype(np.int32)


def _qk_slab_columns(w):
    is_b, dim = _qk_lane_layout()
    heads = w.reshape(*w.shape[:-1], -1, 2, ATT_HEAD_DIM)
    starts = [0] + [i for i in range(1, LANES) if is_b[i] != is_b[i - 1] or dim[i] != dim[i - 1] + 1] + [LANES]
    runs = [heads[..., int(is_b[lo]), int(dim[lo]):int(dim[lo]) + hi - lo] for lo, hi in zip(starts[:-1], starts[1:])]
    return jnp.concatenate(runs, axis=-1).reshape(w.shape)


def _rope_tables_kernel(rfreq_ref, afreq_ref, asign_ref, rcos_ref, rsin_ref, *att_refs):
    seq = rcos_ref.shape[0]
    row = lax.broadcasted_iota(jnp.int32, (seq, LANES), 0)
    rang = row.astype(F32) * rfreq_ref[...]
    rcos_ref[...] = jnp.cos(rang)
    rsin_ref[...] = jnp.sin(rang)
    tm = TOKEN_TILE
    for i, dil in enumerate(DILATIONS):
        per = tm // dil
        in_tile = row & (tm - 1)
        pos = (row - in_tile) + (in_tile & (per - 1)) * dil + lax.shift_right_logical(in_tile, per.bit_length() - 1)
        ang = pos.astype(F32) * afreq_ref[...]
        att_refs[2 * i][...] = jnp.cos(ang)
        att_refs[2 * i + 1][...] = jnp.sin(ang) * asign_ref[...]


def _rope_tables(seq):
    rfreq = 1.0 / (RET_ROT_BASE ** jnp.linspace(0.0, 1.0, RET_QK_DIM // 2, dtype=F32))
    inv = ROPE_THETA ** (-jnp.arange(0, ROPE_DIMS, 2, dtype=F32) / ROPE_DIMS)
    half = ROPE_DIMS // 2
    _, dim = _qk_lane_layout()
    afreq = jnp.where(dim < ROPE_DIMS, inv[dim % half], 0.0).astype(F32)
    asign = jnp.asarray(np.where(dim < half, -1.0, np.where(dim < ROPE_DIMS, 1.0, 0.0)), F32)
    tab = jax.ShapeDtypeStruct((seq, LANES), F32)
    tabs = pl.pallas_call(
        _rope_tables_kernel,
        out_shape=(tab,) * (2 + 2 * N_GROUPS),
        name="rope_tables",
    )(rfreq[None, :], afreq[None, :], asign[None, :])
    return tabs[0], tabs[1], tabs[2:]


def _partial_rotary(t, cos, signed_sin):
    return t * cos + pltpu.roll(t, ATT_HEAD_DIM, 1) * signed_sin


def _ret_mixer_kernel(lg_ref, x_ref, gain_ref, wq_ref, wk_ref, wv_ref, wg_ref, cos_ref, sin_ref, gn_ref, y_ref,
                      state_ref):
    head = pl.program_id(1)
    lg = lg_ref[head]
    seq = x_ref.shape[0]
    pt, ck = RET_PROJ_ROWS, RET_CHUNK
    half = RET_QK_DIM // 2

    n1 = (lax.broadcasted_iota(jnp.int32, (pt, 1), 0) % ck + 1).astype(F32)
    q_scale = jnp.exp(n1 * lg)
    k_scale = jnp.exp(-n1 * lg) * RET_QK_DIM ** -0.5
    causal = lax.broadcasted_iota(jnp.int32, (ck, ck), 0) >= lax.broadcasted_iota(jnp.int32, (ck, ck), 1)
    chunk_decay = jnp.exp(jnp.full((1, RET_V_DIM), ck, F32) * lg)
    gn = gn_ref[...]

    def project(t):
        rows = slice(t * pt, (t + 1) * pt)
        hb = _rms(x_ref[rows, :], gain_ref[...]).astype(BF16)
        cos, sin = cos_ref[rows, :], sin_ref[rows, :]

        def rotated(acc, scale):
            a, b = acc[:, :half], acc[:, half:]
            return (a * cos - b * sin) * scale, (b * cos + a * sin) * scale

        q_lo, q_hi = rotated(_dot(hb, wq_ref[...]), q_scale)
        k_lo, k_hi = rotated(_dot(hb, wk_ref[...]), k_scale)
        q = jnp.concatenate([q_lo, q_hi], axis=1).astype(BF16)
        kt = jnp.concatenate([k_lo.T, k_hi.T], axis=0).astype(BF16)
        v = _dot(hb, wv_ref[...]).astype(BF16)
        gate = _dot(hb, wg_ref[...])
        g = (gate / (1.0 + jnp.exp(-gate)) * gn).astype(BF16)
        return q, kt, v, g

    def state_free(c, q, kt, v, g, j):
        rows = slice(j * ck, (j + 1) * ck)
        qc, ktc, vc = q[rows], kt[:, rows], v[rows]
        scores = jnp.where(causal, _dot(qc, ktc), 0.0)
        intra = _dot(scores.astype(BF16), vc)
        upd = _dot(ktc, vc) if (c + 1) * ck < seq else None
        return c, qc, intra, upd, g[rows]

    def finish(c, qc, intra, upd, gc):
        y = intra + _dot(qc, state_ref[...].astype(BF16))
        if upd is not None:
            state_ref[...] = (state_ref[...] + upd) * chunk_decay
        mu = jnp.mean(y, axis=-1, keepdims=True)
        yc = y - mu
        var = jnp.mean(yc * yc, axis=-1, keepdims=True)
        yn = (yc * lax.rsqrt(var + GN_EPS)).astype(BF16)
        y_ref[c * ck:(c + 1) * ck, :] = yn * gc

    state_ref[...] = jnp.zeros_like(state_ref)
    n_tiles = seq // pt
    ahead, pending = project(0), []
    for t in range(n_tiles):
        current, ahead = ahead, (project(t + 1) if t + 1 < n_tiles else None)
        for j in range(pt // ck):
            pending.append(state_free(t * (pt // ck) + j, *current, j))
            if len(pending) > RET_SKEW:
                finish(*pending.pop(0))
    for item in pending:
        finish(*item)


def _ret_mixer(x, gain, w, layer, cos, sin, gn_gain, log_gamma, batch, seq):
    tokens, d = x.shape
    dqk = RET_HEADS * RET_QK_DIM
    dv = RET_HEADS * RET_V_DIM
    assert seq % RET_PROJ_ROWS == 0 and RET_PROJ_ROWS % RET_CHUNK == 0
    qk_blocks = dqk // RET_QK_DIM
    v_blocks = dv // RET_V_DIM
    wq = pl.BlockSpec((None, d, RET_QK_DIM), lambda b, h: (layer, 0, h))
    wk = pl.BlockSpec((None, d, RET_QK_DIM), lambda b, h: (layer, 0, qk_blocks + h))
    wv = pl.BlockSpec((None, d, RET_V_DIM), lambda b, h: (layer, 0, 2 * dqk // RET_V_DIM + h))
    wg = pl.BlockSpec((None, d, RET_V_DIM), lambda b, h: (layer, 0, 2 * dqk // RET_V_DIM + v_blocks + h))
    head_cols = pl.BlockSpec((seq, RET_V_DIM), lambda b, h: (b, h))
    vmem = (2 * seq * d * 4 + 2 * d * (2 * RET_QK_DIM + 2 * RET_V_DIM) * 2 + 2 * seq * RET_V_DIM * 2
            + 4 * seq * LANES * 4 + 16 * RET_PROJ_ROWS * RET_V_DIM * 4 + (8 << 20))
    return pl.pallas_call(
        _ret_mixer_kernel,
        grid=(batch, RET_HEADS),
        in_specs=[pl.BlockSpec(memory_space=pltpu.SMEM), pl.BlockSpec((seq, d), lambda b, h: (b, 0)),
                  _resident((1, d)), wq, wk, wv, wg, _resident((seq, LANES)), _resident((seq, LANES)),
                  pl.BlockSpec((1, RET_V_DIM), lambda b, h: (0, h))],
        out_specs=head_cols,
        out_shape=jax.ShapeDtypeStruct((tokens, dv), BF16),
        scratch_shapes=[pltpu.VMEM((RET_QK_DIM, RET_V_DIM), F32)],
        compiler_params=_params(2, vmem),
        name="ret_mixer",
    )(log_gamma, x, gain, w, w, w, w, cos, sin, gn_gain)


def _outproj_kernel(x_ref, a_ref, w_ref, gain_ref, o_ref):
    m = _dot(a_ref[...], w_ref[...])
    o_ref[...] = x_ref[...] + _rms(m, gain_ref[...])


def _outproj(x, a, w, layer, gain):
    tokens, d = x.shape
    kdim = a.shape[1]
    tm = WIDE_TILE
    row = lambda i: (i, 0)
    vmem = 4 * tm * d * 4 + 2 * tm * kdim * 2 + kdim * d * 2 + 4 * tm * d * 4 + (8 << 20)
    return pl.pallas_call(
        _outproj_kernel,
        grid=(tokens // tm,),
        in_specs=[pl.BlockSpec((tm, d), row), pl.BlockSpec((tm, kdim), row), _resident(w.shape, layer),
                  _resident((1, d))],
        out_specs=pl.BlockSpec((tm, d), row),
        out_shape=jax.ShapeDtypeStruct((tokens, d), F32),
        compiler_params=_params(1, vmem),
        name="outproj",
    )(x, a, w, gain)


def _natural_rows(src_ref, row0, rows, lane0, stage_ref, slab, mid_slab, dilation):
    per, first = rows // dilation, row0 // dilation
    piece = lambda r: src_ref[r, first:first + per, lane0:lane0 + LANES].astype(F32)
    if dilation == 1:
        return piece(0)
    if dilation <= RELAYOUT_STRIDE:
        for r in range(dilation):
            stage_ref[slab, pl.ds(r, per, stride=dilation), :] = piece(r)
        return stage_ref[slab]
    inner, outer = RELAYOUT_STRIDE, dilation // RELAYOUT_STRIDE
    per_inner = rows // inner
    for r in range(dilation):
        rp, r1 = r % inner, r // inner
        stage_ref[mid_slab, pl.ds(rp * per_inner + r1, per, stride=outer), :] = piece(r)
    for rp in range(inner):
        plane = stage_ref[mid_slab, rp * per_inner:(rp + 1) * per_inner, :]
        stage_ref[slab, pl.ds(rp, per_inner, stride=inner), :] = plane
    return stage_ref[slab]


def _att_merge_kernel(x_ref, o0_ref, o1_ref, o2_ref, m0_ref, m1_ref, m2_ref, d0_ref, d1_ref, d2_ref,
                      expand_ref, w_ref, gain_ref, out_ref, stage_ref, merged_ref):
    o_refs = (o0_ref, o1_ref, o2_ref)
    tm = x_ref.shape[0]
    sub = min(MERGE_SUB, tm)
    n_sub = tm // sub
    n_slabs = ATT_GROUP_WIDTH // LANES
    per_sub = stage_ref.shape[0] // n_sub

    def weights(s):
        nat = lambda ref, k, g: _natural_rows(ref, s * sub, sub, 0, stage_ref, s * per_sub + k * N_GROUPS + g,
                                              s * per_sub + k * N_GROUPS, DILATIONS[g])
        ms = [nat(r, 0, g) for g, r in enumerate((m0_ref, m1_ref, m2_ref))]
        dens = [nat(r, 1, g) for g, r in enumerate((d0_ref, d1_ref, d2_ref))]
        top = jnp.maximum(jnp.maximum(ms[0], ms[1]), ms[2])
        es = [jnp.exp2(m - top) for m in ms]
        inv = 1.0 / (es[0] * dens[0] + es[1] * dens[1] + es[2] * dens[2])
        lane = lax.broadcasted_iota(jnp.int32, top.shape, 1)
        packed = None
        for g, e in enumerate(es):
            alpha = jnp.where(lane < ATT_HEADS, e * inv, 0.0)
            hi = alpha.astype(BF16).astype(F32)
            lo = alpha - hi
            for part, piece in enumerate((hi, lo)):
                shift = (2 * g + part) * ATT_HEADS
                piece = pltpu.roll(piece, shift, 1) if shift else piece
                packed = piece if packed is None else packed + piece
        return packed.astype(BF16)

    def finish(s, packed):
        rows = slice(s * sub, (s + 1) * sub)
        for j in range(n_slabs):
            acc = None
            wide = _dot(packed, expand_ref[:, j * N_GROUPS * LANES:(j + 1) * N_GROUPS * LANES])
            for g in range(N_GROUPS):
                base = s * per_sub + 2 * N_GROUPS + j
                o = _natural_rows(o_refs[g], s * sub, sub, j * LANES, stage_ref, base + g * n_slabs, base,
                                  DILATIONS[g])
                term = wide[:, g * LANES:(g + 1) * LANES] * o
                acc = term if acc is None else acc + term
            merged_ref[rows, j * LANES:(j + 1) * LANES] = acc.astype(BF16)
        m = _dot(merged_ref[rows, :], w_ref[...])
        out_ref[rows, :] = x_ref[rows, :] + _rms(m, gain_ref[...])

    ahead = weights(0)
    for s in range(n_sub):
        current, ahead = ahead, (weights(s + 1) if s + 1 < n_sub else None)
        finish(s, current)


def _att_merge(x, outs, maxes, dens, w, layer, gain, seq):
    tokens, d = x.shape
    tm = WIDE_TILE
    sub = min(MERGE_SUB, tm)
    tps = seq // tm
    row = lambda i: (i, 0)
    gw = ATT_GROUP_WIDTH

    def plane(width, dil):
        return pl.BlockSpec((None, dil, tm // dil, width), lambda i: (i // tps, 0, i % tps, 0))

    col = jnp.arange(N_GROUPS * gw)
    slab_j, group_g, lane_l = col // (N_GROUPS * LANES), (col // LANES) % N_GROUPS, col % LANES
    src = 2 * ATT_HEADS * group_g + (slab_j * LANES + lane_l) // ATT_HEAD_DIM
    lane = jnp.arange(LANES)[:, None]
    expand = ((lane == src[None, :]) | (lane == src[None, :] + ATT_HEADS)).astype(BF16)
    n_stage = 2 * N_GROUPS + N_GROUPS * (gw // LANES)
    vmem = (4 * tm * d * 4 + 6 * tm * gw * 2 + 12 * tm * LANES * 4 + gw * d * 2 + expand.size * 2
            + n_stage * tm * LANES * 4 + 8 * tm * d * 4 + (8 << 20))
    return pl.pallas_call(
        _att_merge_kernel,
        grid=(tokens // tm,),
        in_specs=[pl.BlockSpec((tm, d), row)] + [plane(gw, dil) for dil in DILATIONS]
                 + [plane(LANES, dil) for dil in DILATIONS] * 2
                 + [_resident(expand.shape), _resident(w.shape, layer), _resident((1, d))],
        out_specs=pl.BlockSpec((tm, d), row),
        out_shape=jax.ShapeDtypeStruct((tokens, d), F32),
        scratch_shapes=[pltpu.VMEM((n_stage * (tm // sub), sub, LANES), F32), pltpu.VMEM((tm, gw), BF16)],
        compiler_params=_params(1, vmem),
        name="att_merge",
    )(x, *outs, *maxes, *dens, expand, w, gain)


def _ffn_kernel(x_ref, halo_ref, gpre_ref, wup_ref, cw_ref, cb_ref, wdown_ref, gpost_ref, o_ref, *, tiles_per_seq):
    tm = x_ref.shape[0]
    sub = min(FFN_SUB, tm)
    n_sub = tm // sub
    d_ff = wdown_ref.shape[0]
    gpre = gpre_ref[...]
    seq_start = (pl.program_id(0) % tiles_per_seq) == 0
    ext_row = lax.broadcasted_iota(jnp.int32, (CONV_HALO + sub, 1), 0)
    keep = jnp.logical_or(ext_row >= CONV_HALO, jnp.logical_not(seq_start))
    cw, cb = cw_ref[...], cb_ref[...]

    def up(s):
        h = _rms(x_ref[s * sub:(s + 1) * sub, :], gpre)
        if s == 0:
            before = _rms(halo_ref[...], gpre)
        else:
            before = _rms(x_ref[s * sub - CONV_HALO:s * sub, :], gpre)
        hb_ext = jnp.concatenate([before, h], axis=0).astype(BF16)
        gate = _dot(hb_ext, wup_ref[:, :d_ff])
        if s == 0:
            gate = jnp.where(keep, gate, 0.0)
        return gate, _dot(hb_ext[CONV_HALO:], wup_ref[:, d_ff:])

    def mix(gate, val):
        conv = cb
        for tap in reversed(range(CONV_WIDTH)):
            first = CONV_HALO - (CONV_WIDTH - 1 - tap)
            conv = conv + cw[tap:tap + 1] * gate[first:first + sub]
        gelu = 0.5 * conv * (1.0 + jnp.tanh(math.sqrt(2.0 / math.pi) * (conv + 0.044715 * (conv * conv * conv))))
        return _dot((gelu * val).astype(BF16), wdown_ref[...])

    def finish(s, acc):
        rows = slice(s * sub, (s + 1) * sub)
        o_ref[rows, :] = x_ref[rows, :] + _rms(acc, gpost_ref[...])

    ahead, behind = up(0), None
    for s in range(n_sub):
        current, ahead = ahead, (up(s + 1) if s + 1 < n_sub else None)
        acc = mix(*current)
        if behind is not None:
            finish(*behind)
        behind = (s, acc)
    finish(*behind)


def _ffn(x, gpre, wup, cw, cb, wdown, layer, gpost, seq):
    tokens, d = x.shape
    d_ff = wdown.shape[1]
    tm = WIDE_TILE
    tps = seq // tm
    halo_blocks = tm // CONV_HALO
    row = lambda i: (i, 0)
    halo_spec = pl.BlockSpec((CONV_HALO, d), lambda i: (jnp.maximum(i * halo_blocks - 1, 0), 0))
    vmem = 4 * tm * d * 4 + 3 * d * d_ff * 2 + 6 * tm * d * 4 + 16 * tm * d_ff * 4 + (8 << 20)
    return pl.pallas_call(
        functools.partial(_ffn_kernel, tiles_per_seq=tps),
        grid=(tokens // tm,),
        in_specs=[pl.BlockSpec((tm, d), row), halo_spec, _resident((1, d)), _resident(wup.shape, layer),
                  _resident(cw.shape), _resident((1, d_ff)), _resident(wdown.shape, layer), _resident((1, d))],
        out_specs=pl.BlockSpec((tm, d), row),
        out_shape=jax.ShapeDtypeStruct((tokens, d), F32),
        compiler_params=_params(1, vmem),
        name="conv_ffn",
    )(x, x, gpre, wup, cw, cb, wdown, gpost)


def _att_proj_kernel(x_ref, gain_ref, w_ref, *refs, dilations, n_rope):
    n_tab = 2 * N_GROUPS
    tables, outs = refs[:n_tab], refs[n_tab:n_tab + len(dilations)]
    stage_ref, perm_ref = refs[n_tab + len(dilations):]
    tm, d = x_ref.shape
    h = _rms(x_ref[...], gain_ref[...])
    lhs = {1: h.astype(BF16)}
    for j in range(d // LANES):
        stage_ref[0, j] = h[:, j * LANES:(j + 1) * LANES]
    permuted = sorted(set(dilations) - {1})
    src_dil = 1
    for pi, dil in enumerate(permuted):
        step, per, per_src = dil // src_dil, tm // dil, tm // src_dil
        for j in range(d // LANES):
            for rp in range(src_dil):
                for r1 in range(step):
                    piece = stage_ref[pi, j, pl.ds(rp * per_src + r1, per, stride=step), :]
                    rows = slice((r1 * src_dil + rp) * per, (r1 * src_dil + rp + 1) * per)
                    if pi + 1 < len(permuted):
                        stage_ref[pi + 1, j, rows, :] = piece
                    perm_ref[pi, rows, j * LANES:(j + 1) * LANES] = piece.astype(BF16)
        lhs[dil] = perm_ref.at[pi]
        src_dil = dil
    work = [(s, c0) for s in sorted(range(len(outs)), key=lambda i: dilations[i])
            for c0 in range(0, ATT_GROUP_WIDTH, COL_CHUNK)]

    def project(s, c0):
        dil = dilations[s]
        col = s * ATT_GROUP_WIDTH + c0
        return _dot(lhs[dil] if dil == 1 else lhs[dil][...], w_ref[:, col:col + COL_CHUNK])

    ahead = project(*work[0])
    for i, (s, c0) in enumerate(work):
        acc, ahead = ahead, (project(*work[i + 1]) if i + 1 < len(work) else None)
        o_ref, dil = outs[s], dilations[s]
        gi = DILATIONS.index(dil)
        cos, signed_sin = tables[2 * gi][...], tables[2 * gi + 1][...]
        per = tm // dil
        for j in range(COL_CHUNK // LANES):
            t = acc[:, j * LANES:(j + 1) * LANES]
            if s < n_rope:
                t = _partial_rotary(t, cos, signed_sin)
            t = t.astype(BF16)
            lane0 = c0 + j * LANES
            for r in range(dil):
                o_ref[r, :, lane0:lane0 + LANES] = t[r * per:(r + 1) * per]


def _att_proj(x, gain, w, layer, tables, n_rope, batch, seq):
    tokens, d = x.shape
    n = w.shape[-1]
    gw = ATT_GROUP_WIDTH
    dilations = DILATIONS * (n // (N_GROUPS * gw))
    n_perm = len(set(dilations) - {1})
    tm = TOKEN_TILE
    tps = seq // tm
    row = lambda i: (i, 0)
    tab = pl.BlockSpec((tm, LANES), lambda i: (i % tps, 0))
    plane = lambda dil: pl.BlockSpec((None, dil, tm // dil, gw), lambda i: (i // tps, 0, i % tps, 0))
    vmem = (2 * tm * d * 4 + d * n * 2 + 2 * tm * n * 2 + 8 * tm * COL_CHUNK * 4 + 2 * len(tables) * tm * LANES * 4
            + n_perm * tm * d * (4 + 2) + (8 << 20))
    return pl.pallas_call(
        functools.partial(_att_proj_kernel, dilations=dilations, n_rope=n_rope),
        grid=(tokens // tm,),
        in_specs=[pl.BlockSpec((tm, d), row), _resident((1, d)), _resident(w.shape, layer)] + [tab] * len(tables),
        out_specs=[plane(dil) for dil in dilations],
        out_shape=[jax.ShapeDtypeStruct((batch, dil, seq // dil, gw), BF16) for dil in dilations],
        scratch_shapes=[pltpu.VMEM((n_perm, d // LANES, tm, LANES), F32), pltpu.VMEM((n_perm, tm, d), BF16)],
        compiler_params=_params(1, vmem),
        name="att_proj",
    )(x, gain, w, *tables)


def _dil_attn_kernel(q_ref, k_ref, v_ref, o_ref, m_ref, d_ref, bias_first_ref, bias_band_ref):
    n_planes, length = q_ref.shape[0], q_ref.shape[1]
    blk = ATT_BLOCK
    n_blocks = length // blk
    n_pairs = q_ref.shape[2] // LANES
    lane = lax.broadcasted_iota(jnp.int32, (blk, LANES), 1)
    low = lane < ATT_HEAD_DIM
    is_b, _ = _qk_lane_layout()
    edges = [0] + [i for i in range(1, LANES) if is_b[i] != is_b[i - 1]] + [LANES]
    q_first = None
    for lo, hi in zip(edges[:-1], edges[1:]):
        if not is_b[lo]:
            run = jnp.logical_and(lane >= lo, lane < hi)
            q_first = run if q_first is None else jnp.logical_or(q_first, run)
    zero = jnp.zeros((blk, LANES), BF16)

    def bias(n_keys):
        qi = lax.broadcasted_iota(jnp.int32, (2 * blk, n_keys), 0) % blk
        kj = lax.broadcasted_iota(jnp.int32, (2 * blk, n_keys), 1)
        dist = qi + (n_keys - blk) - kj
        return jnp.where(jnp.logical_and(dist >= 0, dist <= blk), 0.0, NEG_BIG).astype(F32)

    bias_first_ref[...] = bias(blk)
    bias_band_ref[...] = bias(2 * blk)

    def scores(plane, q_rows, k_rows, bias_ref, hp):
        cols = slice(hp * LANES, (hp + 1) * LANES)
        q2 = q_ref[plane, q_rows, cols]
        qq = jnp.concatenate([jnp.where(q_first, q2, zero), jnp.where(q_first, zero, q2)], axis=0)
        return _dot_nt(qq, k_ref[plane, k_rows, cols]) + bias_ref[...]

    def softmax(s):
        m = jnp.max(s, axis=-1, keepdims=True)
        return m, jnp.exp2(s - m).astype(BF16)

    def finish(plane, q_rows, k_rows, bias_ref, hp, mp):
        m, p = mp
        cols = slice(hp * LANES, (hp + 1) * LANES)
        ones = jnp.ones((bias_ref.shape[1], LANES), BF16)
        pv = _dot(p, jnp.concatenate([v_ref[plane, k_rows, cols], ones], axis=1))
        o_ref[plane, q_rows, cols] = jnp.where(low, pv[:blk, :LANES], pv[blk:, :LANES]).astype(BF16)
        for half in range(2):
            head = 2 * hp + half
            rows = slice(half * blk, (half + 1) * blk)
            m_ref[plane, q_rows, head:head + 1] = m[rows]
            d_ref[plane, q_rows, head:head + 1] = pv[rows, LANES + head:LANES + head + 1]

    def run(items):
        work = [(*item, hp) for item in items for hp in range(n_pairs)]
        for plane, q_rows, _, _ in items:
            m_ref[plane, q_rows, :] = jnp.zeros((blk, LANES), F32)
            d_ref[plane, q_rows, :] = jnp.ones((blk, LANES), F32)
        n = len(work)
        s_vals, mp_vals = {}, {}
        for t in range(-ATT_SKEW, n):
            if t + ATT_SKEW < n:
                s_vals[t + ATT_SKEW] = scores(*work[t + ATT_SKEW])
            mid = t + ATT_SKEW // 2
            if 0 <= mid < n and ATT_SKEW > 1:
                mp_vals[mid] = softmax(s_vals.pop(mid))
            if t >= 0:
                if ATT_SKEW <= 1:
                    mp_vals[t] = softmax(s_vals.pop(t))
                finish(*work[t], mp_vals.pop(t))

    run([(plane, pl.ds(0, blk), pl.ds(0, blk), bias_first_ref) for plane in range(n_planes)])
    n_band = n_blocks - 1
    unroll = max(u for u in range(1, ATT_UNROLL + 1) if n_band % u == 0) if n_band else 0
    for plane in range(n_planes if n_band else 0):
        def later_blocks(i, carry, plane=plane):
            items = []
            for j in range(unroll):
                start = (i * unroll + j + 1) * blk
                start = start if isinstance(start, int) else pl.multiple_of(start, blk)
                items.append((plane, pl.ds(start, blk), pl.ds(start - blk, 2 * blk), bias_band_ref))
            run(items)
            return carry

        if n_band == unroll:
            later_blocks(0, 0)
        else:
            lax.fori_loop(0, n_band // unroll, later_blocks, 0)


def _dil_attn(q, k, v):
    batch, dil, length, gw = q.shape
    planes = min(dil, max(1, ATT_MIN_BLOCKS * ATT_BLOCK // length))
    spec = lambda width: pl.BlockSpec((None, planes, length, width), lambda b, r: (b, r, 0, 0))
    blk = ATT_BLOCK
    vmem = (2 * 4 * planes * length * gw * 2 + 4 * planes * length * LANES * 4 + 3 * 2 * blk * 2 * blk * 4
            + 48 * blk * 2 * blk * 4 + (8 << 20))
    stat = jax.ShapeDtypeStruct((batch, dil, length, LANES), F32)
    return pl.pallas_call(
        _dil_attn_kernel,
        grid=(batch, dil // planes),
        in_specs=[spec(gw), spec(gw), spec(gw)],
        out_specs=[spec(gw), spec(LANES), spec(LANES)],
        out_shape=[jax.ShapeDtypeStruct((batch, dil, length, gw), BF16), stat, stat],
        scratch_shapes=[pltpu.VMEM((2 * blk, blk), F32), pltpu.VMEM((2 * blk, 2 * blk), F32)],
        compiler_params=_params(2, vmem),
        name=f"dil_attn_d{dil}",
    )(q, k, v)


def kernel(x, ret_w_in, ret_w_out, ret_gn_gain, kv_norm, att_w_kv, att_w_q, att_w_o, norm_mix_pre, norm_mix_post, norm_ffn_pre, norm_ffn_post, ffn_w_up, ffn_conv_w, ffn_conv_b, ffn_w_down):
    batch, seq, d = x.shape
    depth = norm_mix_pre.shape[0]
    n_ret = ret_w_in.shape[0]
    assert seq % TOKEN_TILE == 0 and seq % WIDE_TILE == 0 and seq % RET_CHUNK == 0
    assert all(seq % (dil * ATT_BLOCK) == 0 and TOKEN_TILE % (16 * dil) == 0 for dil in DILATIONS)
    assert all(win // dil == ATT_BLOCK for win, dil in DIL_GROUPS)

    rcos, rsin, att_tables = _rope_tables(seq)
    q_scale = ATT_HEAD_DIM ** -0.5 * math.log2(math.e)
    n_qk = N_GROUPS * ATT_GROUP_WIDTH
    log_gamma = jnp.log(1.0 - 2.0 ** (-5.0 - jnp.arange(RET_HEADS, dtype=F32)))
    gain = lambda g: g[None, :]
    bf = lambda w: w.astype(BF16)
    w_in, w_out, w_o = bf(ret_w_in), bf(ret_w_out), bf(att_w_o)
    w_q = bf(_qk_slab_columns(att_w_q) * q_scale)
    w_kv = bf(jnp.concatenate([_qk_slab_columns(att_w_kv[:, :n_qk]), att_w_kv[:, n_qk:]], axis=1))
    w_up, w_down = bf(ffn_w_up), bf(ffn_w_down)

    xt = x.reshape(batch * seq, d)
    keys = values = None
    for layer in range(depth):
        if layer < n_ret:
            y = _ret_mixer(xt, gain(norm_mix_pre[layer]), w_in, layer, rcos, rsin,
                           gain(ret_gn_gain[layer]), log_gamma, batch, seq)
            xt = _outproj(xt, y, w_out, layer, gain(norm_mix_post[layer]))
        else:
            bi = layer - n_ret
            queries = _att_proj(xt, gain(norm_mix_pre[layer]), w_q, bi, att_tables, N_GROUPS, batch, seq)
            outs, maxes, dens = zip(*[_dil_attn(queries[gi], keys[gi], values[gi]) for gi in range(N_GROUPS)])
            xt = _att_merge(xt, outs, maxes, dens, w_o, bi, gain(norm_mix_post[layer]), seq)
        xt = _ffn(xt, gain(norm_ffn_pre[layer]), w_up, ffn_conv_w[layer], gain(ffn_conv_b[layer]),
                  w_down, layer, gain(norm_ffn_post[layer]), seq)
        if layer == n_ret - 1:
            kv = _att_proj(xt, gain(kv_norm), w_kv, None, att_tables, N_GROUPS, batch, seq)
            keys, values = kv[:N_GROUPS], kv[N_GROUPS:]
    return xt.reshape(batch, seq, d)
```

```python
import functools
import math

import jax
import jax.numpy as jnp
import numpy as np
from jax import lax
from jax.experimental import pallas as pl
from jax.experimental.pallas import tpu as pltpu

F32 = jnp.float32
BF16 = jnp.bfloat16

LANES = 128
V7X_VMEM_BYTES = 64 * 1024 * 1024

RMS_EPS = 1e-6
GN_EPS = 1e-6
RET_HEADS = 4
RET_QK_DIM = 256
RET_V_DIM = 512
RET_ROT_BASE = 10000.0
ATT_HEADS = 16
ATT_HEAD_DIM = 64
ATT_GROUP_WIDTH = ATT_HEADS * ATT_HEAD_DIM
DIL_GROUPS = ((128, 1), (512, 4), (2048, 16))
DILATIONS = tuple(d for _, d in DIL_GROUPS)
N_GROUPS = 3
ATT_BLOCK = 128
ROPE_THETA = 500000.0
ROPE_DIMS = ATT_HEAD_DIM // 4
CONV_WIDTH = 3

TOKEN_TILE = 512
WIDE_TILE = 1024
COL_CHUNK = 512
FFN_SUB = 512
MERGE_SUB = 512
RELAYOUT_STRIDE = 4
CONV_HALO = 16
RET_CHUNK = 256
RET_SKEW = 0
RET_PROJ_ROWS = 256
ATT_MIN_BLOCKS = 16
ATT_SKEW = 4
ATT_UNROLL = 15
NEG_BIG = -1e30


def _vmem_limit(nbytes):
    return int(min(V7X_VMEM_BYTES - V7X_VMEM_BYTES // 16, max(nbytes, V7X_VMEM_BYTES // 4)))


def _params(n_axes, vmem_bytes):
    return pltpu.CompilerParams(dimension_semantics=("arbitrary",) * n_axes,
                                vmem_limit_bytes=_vmem_limit(vmem_bytes))


def _resident(shape, layer=None):
    if layer is None:
        return pl.BlockSpec(shape, lambda *_: (0,) * len(shape), pipeline_mode=pl.Buffered(1))
    return pl.BlockSpec((None,) + tuple(shape[1:]), lambda *_: (layer,) + (0,) * (len(shape) - 1),
                        pipeline_mode=pl.Buffered(1))


def _rms(x, gain):
    return x * lax.rsqrt(jnp.mean(x * x, axis=-1, keepdims=True) + RMS_EPS) * gain


def _dot(a, b):
    return jnp.dot(a, b, preferred_element_type=F32)


def _dot_nt(a, b):
    return lax.dot_general(a, b, (((1,), (1,)), ((), ())), preferred_element_type=F32)


def _qk_lane_layout():
    half = ROPE_DIMS // 2
    lane = np.arange(LANES)
    upper = lane >= ATT_HEAD_DIM
    local = lane - ATT_HEAD_DIM * upper
    is_b = np.where(upper, local >= half, (local >= half) & (local < ROPE_DIMS))
    dim = np.where(local < ROPE_DIMS, local % half + half * upper, local)
    dim = np.where(upper & (local >= ROPE_DIMS), local, dim)
    return is_b.astype(np.int32), dim.astype(np.int32)


def _qk_slab_columns(w):
    is_b, dim = _qk_lane_layout()
    heads = w.reshape(*w.shape[:-1], -1, 2, ATT_HEAD_DIM)
    starts = [0] + [i for i in range(1, LANES) if is_b[i] != is_b[i - 1] or dim[i] != dim[i - 1] + 1] + [LANES]
    runs = [heads[..., int(is_b[lo]), int(dim[lo]):int(dim[lo]) + hi - lo] for lo, hi in zip(starts[:-1], starts[1:])]
    return jnp.concatenate(runs, axis=-1).reshape(w.shape)


def _rope_tables_kernel(rfreq_ref, afreq_ref, asign_ref, rcos_ref, rsin_ref, *att_refs):
    seq = rcos_ref.shape[0]
    row = lax.broadcasted_iota(jnp.int32, (seq, LANES), 0)
    rang = row.astype(F32) * rfreq_ref[...]
    rcos_ref[...] = jnp.cos(rang)
    rsin_ref[...] = jnp.sin(rang)
    tm = TOKEN_TILE
    for i, dil in enumerate(DILATIONS):
        per = tm // dil
        in_tile = row & (tm - 1)
        pos = (row - in_tile) + (in_tile & (per - 1)) * dil + lax.shift_right_logical(in_tile, per.bit_length() - 1)
        ang = pos.astype(F32) * afreq_ref[...]
        att_refs[2 * i][...] = jnp.cos(ang)
        att_refs[2 * i + 1][...] = jnp.sin(ang) * asign_ref[...]


def _rope_tables(seq):
    rfreq = 1.0 / (RET_ROT_BASE ** jnp.linspace(0.0, 1.0, RET_QK_DIM // 2, dtype=F32))
    inv = ROPE_THETA ** (-jnp.arange(0, ROPE_DIMS, 2, dtype=F32) / ROPE_DIMS)
    half = ROPE_DIMS // 2
    _, dim = _qk_lane_layout()
    afreq = jnp.where(dim < ROPE_DIMS, inv[dim % half], 0.0).astype(F32)
    asign = jnp.asarray(np.where(dim < half, -1.0, np.where(dim < ROPE_DIMS, 1.0, 0.0)), F32)
    tab = jax.ShapeDtypeStruct((seq, LANES), F32)
    tabs = pl.pallas_call(
        _rope_tables_kernel,
        out_shape=(tab,) * (2 + 2 * N_GROUPS),
        name="rope_tables",
    )(rfreq[None, :], afreq[None, :], asign[None, :])
    return tabs[0], tabs[1], tabs[2:]


def _partial_rotary(t, cos, signed_sin):
    return t * cos + pltpu.roll(t, ATT_HEAD_DIM, 1) * signed_sin


def _ret_mixer_kernel(lg_ref, x_ref, gain_ref, wq_ref, wk_ref, wv_ref, wg_ref, cos_ref, sin_ref, gn_ref, y_ref,
                      state_ref):
    head = pl.program_id(1)
    lg = lg_ref[head]
    seq = x_ref.shape[0]
    pt, ck = RET_PROJ_ROWS, RET_CHUNK
    half = RET_QK_DIM // 2

    n1 = (lax.broadcasted_iota(jnp.int32, (pt, 1), 0) % ck + 1).astype(F32)
    q_scale = jnp.exp(n1 * lg)
    k_scale = jnp.exp(-n1 * lg) * RET_QK_DIM ** -0.5
    causal = lax.broadcasted_iota(jnp.int32, (ck, ck), 0) >= lax.broadcasted_iota(jnp.int32, (ck, ck), 1)
    chunk_decay = jnp.exp(jnp.full((1, RET_V_DIM), ck, F32) * lg)
    gn = gn_ref[...]

    def project(t):
        rows = slice(t * pt, (t + 1) * pt)
        hb = _rms(x_ref[rows, :], gain_ref[...]).astype(BF16)
        cos, sin = cos_ref[rows, :], sin_ref[rows, :]

        def rotated(acc, scale):
            a, b = acc[:, :half], acc[:, half:]
            return (a * cos - b * sin) * scale, (b * cos + a * sin) * scale

        q_lo, q_hi = rotated(_dot(hb, wq_ref[...]), q_scale)
        k_lo, k_hi = rotated(_dot(hb, wk_ref[...]), k_scale)
        q = jnp.concatenate([q_lo, q_hi], axis=1).astype(BF16)
        kt = jnp.concatenate([k_lo.T, k_hi.T], axis=0).astype(BF16)
        v = _dot(hb, wv_ref[...]).astype(BF16)
        gate = _dot(hb, wg_ref[...])
        g = (gate / (1.0 + jnp.exp(-gate)) * gn).astype(BF16)
        return q, kt, v, g

    def state_free(c, q, kt, v, g, j):
        rows = slice(j * ck, (j + 1) * ck)
        qc, ktc, vc = q[rows], kt[:, rows], v[rows]
        scores = jnp.where(causal, _dot(qc, ktc), 0.0)
        intra = _dot(scores.astype(BF16), vc)
        upd = _dot(ktc, vc) if (c + 1) * ck < seq else None
        return c, qc, intra, upd, g[rows]

    def finish(c, qc, intra, upd, gc):
        y = intra + _dot(qc, state_ref[...].astype(BF16))
        if upd is not None:
            state_ref[...] = (state_ref[...] + upd) * chunk_decay
        mu = jnp.mean(y, axis=-1, keepdims=True)
        yc = y - mu
        var = jnp.mean(yc * yc, axis=-1, keepdims=True)
        yn = (yc * lax.rsqrt(var + GN_EPS)).astype(BF16)
        y_ref[c * ck:(c + 1) * ck, :] = yn * gc

    state_ref[...] = jnp.zeros_like(state_ref)
    n_tiles = seq // pt
    ahead, pending = project(0), []
    for t in range(n_tiles):
        current, ahead = ahead, (project(t + 1) if t + 1 < n_tiles else None)
        for j in range(pt // ck):
            pending.append(state_free(t * (pt // ck) + j, *current, j))
            if len(pending) > RET_SKEW:
                finish(*pending.pop(0))
    for item in pending:
        finish(*item)


def _ret_mixer(x, gain, w, layer, cos, sin, gn_gain, log_gamma, batch, seq):
    tokens, d = x.shape
    dqk = RET_HEADS * RET_QK_DIM
    dv = RET_HEADS * RET_V_DIM
    assert seq % RET_PROJ_ROWS == 0 and RET_PROJ_ROWS % RET_CHUNK == 0
    qk_blocks = dqk // RET_QK_DIM
    v_blocks = dv // RET_V_DIM
    wq = pl.BlockSpec((None, d, RET_QK_DIM), lambda b, h: (layer, 0, h))
    wk = pl.BlockSpec((None, d, RET_QK_DIM), lambda b, h: (layer, 0, qk_blocks + h))
    wv = pl.BlockSpec((None, d, RET_V_DIM), lambda b, h: (layer, 0, 2 * dqk // RET_V_DIM + h))
    wg = pl.BlockSpec((None, d, RET_V_DIM), lambda b, h: (layer, 0, 2 * dqk // RET_V_DIM + v_blocks + h))
    head_cols = pl.BlockSpec((seq, RET_V_DIM), lambda b, h: (b, h))
    vmem = (2 * seq * d * 4 + 2 * d * (2 * RET_QK_DIM + 2 * RET_V_DIM) * 2 + 2 * seq * RET_V_DIM * 2
            + 4 * seq * LANES * 4 + 16 * RET_PROJ_ROWS * RET_V_DIM * 4 + (8 << 20))
    return pl.pallas_call(
        _ret_mixer_kernel,
        grid=(batch, RET_HEADS),
        in_specs=[pl.BlockSpec(memory_space=pltpu.SMEM), pl.BlockSpec((seq, d), lambda b, h: (b, 0)),
                  _resident((1, d)), wq, wk, wv, wg, _resident((seq, LANES)), _resident((seq, LANES)),
                  pl.BlockSpec((1, RET_V_DIM), lambda b, h: (0, h))],
        out_specs=head_cols,
        out_shape=jax.ShapeDtypeStruct((tokens, dv), BF16),
        scratch_shapes=[pltpu.VMEM((RET_QK_DIM, RET_V_DIM), F32)],
        compiler_params=_params(2, vmem),
        name="ret_mixer",
    )(log_gamma, x, gain, w, w, w, w, cos, sin, gn_gain)


def _outproj_kernel(x_ref, a_ref, w_ref, gain_ref, o_ref):
    m = _dot(a_ref[...], w_ref[...])
    o_ref[...] = x_ref[...] + _rms(m, gain_ref[...])


def _outproj(x, a, w, layer, gain):
    tokens, d = x.shape
    kdim = a.shape[1]
    tm = WIDE_TILE
    row = lambda i: (i, 0)
    vmem = 4 * tm * d * 4 + 2 * tm * kdim * 2 + kdim * d * 2 + 4 * tm * d * 4 + (8 << 20)
    return pl.pallas_call(
        _outproj_kernel,
        grid=(tokens // tm,),
        in_specs=[pl.BlockSpec((tm, d), row), pl.BlockSpec((tm, kdim), row), _resident(w.shape, layer),
                  _resident((1, d))],
        out_specs=pl.BlockSpec((tm, d), row),
        out_shape=jax.ShapeDtypeStruct((tokens, d), F32),
        compiler_params=_params(1, vmem),
        name="outproj",
    )(x, a, w, gain)


def _natural_rows(src_ref, row0, rows, lane0, stage_ref, slab, mid_slab, dilation):
    per, first = rows // dilation, row0 // dilation
    piece = lambda r: src_ref[r, first:first + per, lane0:lane0 + LANES].astype(F32)
    if dilation == 1:
        return piece(0)
    if dilation <= RELAYOUT_STRIDE:
        for r in range(dilation):
            stage_ref[slab, pl.ds(r, per, stride=dilation), :] = piece(r)
        return stage_ref[slab]
    inner, outer = RELAYOUT_STRIDE, dilation // RELAYOUT_STRIDE
    per_inner = rows // inner
    for r in range(dilation):
        rp, r1 = r % inner, r // inner
        stage_ref[mid_slab, pl.ds(rp * per_inner + r1, per, stride=outer), :] = piece(r)
    for rp in range(inner):
        plane = stage_ref[mid_slab, rp * per_inner:(rp + 1) * per_inner, :]
        stage_ref[slab, pl.ds(rp, per_inner, stride=inner), :] = plane
    return stage_ref[slab]


def _att_merge_kernel(x_ref, o0_ref, o1_ref, o2_ref, m0_ref, m1_ref, m2_ref, d0_ref, d1_ref, d2_ref,
                      expand_ref, w_ref, gain_ref, out_ref, stage_ref, merged_ref):
    o_refs = (o0_ref, o1_ref, o2_ref)
    tm = x_ref.shape[0]
    sub = min(MERGE_SUB, tm)
    n_sub = tm // sub
    n_slabs = ATT_GROUP_WIDTH // LANES
    per_sub = stage_ref.shape[0] // n_sub

    def weights(s):
        nat = lambda ref, k, g: _natural_rows(ref, s * sub, sub, 0, stage_ref, s * per_sub + k * N_GROUPS + g,
                                              s * per_sub + k * N_GROUPS, DILATIONS[g])
        ms = [nat(r, 0, g) for g, r in enumerate((m0_ref, m1_ref, m2_ref))]
        dens = [nat(r, 1, g) for g, r in enumerate((d0_ref, d1_ref, d2_ref))]
        top = jnp.maximum(jnp.maximum(ms[0], ms[1]), ms[2])
        es = [jnp.exp2(m - top) for m in ms]
        inv = 1.0 / (es[0] * dens[0] + es[1] * dens[1] + es[2] * dens[2])
        lane = lax.broadcasted_iota(jnp.int32, top.shape, 1)
        packed = None
        for g, e in enumerate(es):
            alpha = jnp.where(lane < ATT_HEADS, e * inv, 0.0)
            hi = alpha.astype(BF16).astype(F32)
            lo = alpha - hi
            for part, piece in enumerate((hi, lo)):
                shift = (2 * g + part) * ATT_HEADS
                piece = pltpu.roll(piece, shift, 1) if shift else piece
                packed = piece if packed is None else packed + piece
        return packed.astype(BF16)

    def finish(s, packed):
        rows = slice(s * sub, (s + 1) * sub)
        for j in range(n_slabs):
            acc = None
            wide = _dot(packed, expand_ref[:, j * N_GROUPS * LANES:(j + 1) * N_GROUPS * LANES])
            for g in range(N_GROUPS):
                base = s * per_sub + 2 * N_GROUPS + j
                o = _natural_rows(o_refs[g], s * sub, sub, j * LANES, stage_ref, base + g * n_slabs, base,
                                  DILATIONS[g])
                term = wide[:, g * LANES:(g + 1) * LANES] * o
                acc = term if acc is None else acc + term
            merged_ref[rows, j * LANES:(j + 1) * LANES] = acc.astype(BF16)
        m = _dot(merged_ref[rows, :], w_ref[...])
        out_ref[rows, :] = x_ref[rows, :] + _rms(m, gain_ref[...])

    ahead = weights(0)
    for s in range(n_sub):
        current, ahead = ahead, (weights(s + 1) if s + 1 < n_sub else None)
        finish(s, current)


def _att_merge(x, outs, maxes, dens, w, layer, gain, seq):
    tokens, d = x.shape
    tm = WIDE_TILE
    sub = min(MERGE_SUB, tm)
    tps = seq // tm
    row = lambda i: (i, 0)
    gw = ATT_GROUP_WIDTH

    def plane(width, dil):
        return pl.BlockSpec((None, dil, tm // dil, width), lambda i: (i // tps, 0, i % tps, 0))

    col = jnp.arange(N_GROUPS * gw)
    slab_j, group_g, lane_l = col // (N_GROUPS * LANES), (col // LANES) % N_GROUPS, col % LANES
    src = 2 * ATT_HEADS * group_g + (slab_j * LANES + lane_l) // ATT_HEAD_DIM
    lane = jnp.arange(LANES)[:, None]
    expand = ((lane == src[None, :]) | (lane == src[None, :] + ATT_HEADS)).astype(BF16)
    n_stage = 2 * N_GROUPS + N_GROUPS * (gw // LANES)
    vmem = (4 * tm * d * 4 + 6 * tm * gw * 2 + 12 * tm * LANES * 4 + gw * d * 2 + expand.size * 2
            + n_stage * tm * LANES * 4 + 8 * tm * d * 4 + (8 << 20))
    return pl.pallas_call(
        _att_merge_kernel,
        grid=(tokens // tm,),
        in_specs=[pl.BlockSpec((tm, d), row)] + [plane(gw, dil) for dil in DILATIONS]
                 + [plane(LANES, dil) for dil in DILATIONS] * 2
                 + [_resident(expand.shape), _resident(w.shape, layer), _resident((1, d))],
        out_specs=pl.BlockSpec((tm, d), row),
        out_shape=jax.ShapeDtypeStruct((tokens, d), F32),
        scratch_shapes=[pltpu.VMEM((n_stage * (tm // sub), sub, LANES), F32), pltpu.VMEM((tm, gw), BF16)],
        compiler_params=_params(1, vmem),
        name="att_merge",
    )(x, *outs, *maxes, *dens, expand, w, gain)


def _ffn_kernel(x_ref, halo_ref, gpre_ref, wup_ref, cw_ref, cb_ref, wdown_ref, gpost_ref, o_ref, *, tiles_per_seq):
    tm = x_ref.shape[0]
    sub = min(FFN_SUB, tm)
    n_sub = tm // sub
    d_ff = wdown_ref.shape[0]
    gpre = gpre_ref[...]
    seq_start = (pl.program_id(0) % tiles_per_seq) == 0
    ext_row = lax.broadcasted_iota(jnp.int32, (CONV_HALO + sub, 1), 0)
    keep = jnp.logical_or(ext_row >= CONV_HALO, jnp.logical_not(seq_start))
    cw, cb = cw_ref[...], cb_ref[...]

    def up(s):
        h = _rms(x_ref[s * sub:(s + 1) * sub, :], gpre)
        if s == 0:
            before = _rms(halo_ref[...], gpre)
        else:
            before = _rms(x_ref[s * sub - CONV_HALO:s * sub, :], gpre)
        hb_ext = jnp.concatenate([before, h], axis=0).astype(BF16)
        gate = _dot(hb_ext, wup_ref[:, :d_ff])
        if s == 0:
            gate = jnp.where(keep, gate, 0.0)
        return gate, _dot(hb_ext[CONV_HALO:], wup_ref[:, d_ff:])

    def mix(gate, val):
        conv = cb
        for tap in reversed(range(CONV_WIDTH)):
            first = CONV_HALO - (CONV_WIDTH - 1 - tap)
            conv = conv + cw[tap:tap + 1] * gate[first:first + sub]
        gelu = 0.5 * conv * (1.0 + jnp.tanh(math.sqrt(2.0 / math.pi) * (conv + 0.044715 * (conv * conv * conv))))
        return _dot((gelu * val).astype(BF16), wdown_ref[...])

    def finish(s, acc):
        rows = slice(s * sub, (s + 1) * sub)
        o_ref[rows, :] = x_ref[rows, :] + _rms(acc, gpost_ref[...])

    ahead, behind = up(0), None
    for s in range(n_sub):
        current, ahead = ahead, (up(s + 1) if s + 1 < n_sub else None)
        acc = mix(*current)
        if behind is not None:
            finish(*behind)
        behind = (s, acc)
    finish(*behind)


def _ffn(x, gpre, wup, cw, cb, wdown, layer, gpost, seq):
    tokens, d = x.shape
    d_ff = wdown.shape[1]
    tm = WIDE_TILE
    tps = seq // tm
    halo_blocks = tm // CONV_HALO
    row = lambda i: (i, 0)
    halo_spec = pl.BlockSpec((CONV_HALO, d), lambda i: (jnp.maximum(i * halo_blocks - 1, 0), 0))
    vmem = 4 * tm * d * 4 + 3 * d * d_ff * 2 + 6 * tm * d * 4 + 16 * tm * d_ff * 4 + (8 << 20)
    return pl.pallas_call(
        functools.partial(_ffn_kernel, tiles_per_seq=tps),
        grid=(tokens // tm,),
        in_specs=[pl.BlockSpec((tm, d), row), halo_spec, _resident((1, d)), _resident(wup.shape, layer),
                  _resident(cw.shape), _resident((1, d_ff)), _resident(wdown.shape, layer), _resident((1, d))],
        out_specs=pl.BlockSpec((tm, d), row),
        out_shape=jax.ShapeDtypeStruct((tokens, d), F32),
        compiler_params=_params(1, vmem),
        name="conv_ffn",
    )(x, x, gpre, wup, cw, cb, wdown, gpost)


def _att_proj_kernel(x_ref, gain_ref, w_ref, *refs, dilations, n_rope):
    n_tab = 2 * N_GROUPS
    tables, outs = refs[:n_tab], refs[n_tab:n_tab + len(dilations)]
    stage_ref, perm_ref = refs[n_tab + len(dilations):]
    tm, d = x_ref.shape
    h = _rms(x_ref[...], gain_ref[...])
    lhs = {1: h.astype(BF16)}
    for j in range(d // LANES):
        stage_ref[0, j] = h[:, j * LANES:(j + 1) * LANES]
    permuted = sorted(set(dilations) - {1})
    src_dil = 1
    for pi, dil in enumerate(permuted):
        step, per, per_src = dil // src_dil, tm // dil, tm // src_dil
        for j in range(d // LANES):
            for rp in range(src_dil):
                for r1 in range(step):
                    piece = stage_ref[pi, j, pl.ds(rp * per_src + r1, per, stride=step), :]
                    rows = slice((r1 * src_dil + rp) * per, (r1 * src_dil + rp + 1) * per)
                    if pi + 1 < len(permuted):
                        stage_ref[pi + 1, j, rows, :] = piece
                    perm_ref[pi, rows, j * LANES:(j + 1) * LANES] = piece.astype(BF16)
        lhs[dil] = perm_ref.at[pi]
        src_dil = dil
    work = [(s, c0) for s in sorted(range(len(outs)), key=lambda i: dilations[i])
            for c0 in range(0, ATT_GROUP_WIDTH, COL_CHUNK)]

    def project(s, c0):
        dil = dilations[s]
        col = s * ATT_GROUP_WIDTH + c0
        return _dot(lhs[dil] if dil == 1 else lhs[dil][...], w_ref[:, col:col + COL_CHUNK])

    ahead = project(*work[0])
    for i, (s, c0) in enumerate(work):
        acc, ahead = ahead, (project(*work[i + 1]) if i + 1 < len(work) else None)
        o_ref, dil = outs[s], dilations[s]
        gi = DILATIONS.index(dil)
        cos, signed_sin = tables[2 * gi][...], tables[2 * gi + 1][...]
        per = tm // dil
        for j in range(COL_CHUNK // LANES):
            t = acc[:, j * LANES:(j + 1) * LANES]
            if s < n_rope:
                t = _partial_rotary(t, cos, signed_sin)
            t = t.astype(BF16)
            lane0 = c0 + j * LANES
            for r in range(dil):
                o_ref[r, :, lane0:lane0 + LANES] = t[r * per:(r + 1) * per]


def _att_proj(x, gain, w, layer, tables, n_rope, batch, seq):
    tokens, d = x.shape
    n = w.shape[-1]
    gw = ATT_GROUP_WIDTH
    dilations = DILATIONS * (n // (N_GROUPS * gw))
    n_perm = len(set(dilations) - {1})
    tm = TOKEN_TILE
    tps = seq // tm
    row = lambda i: (i, 0)
    tab = pl.BlockSpec((tm, LANES), lambda i: (i % tps, 0))
    plane = lambda dil: pl.BlockSpec((None, dil, tm // dil, gw), lambda i: (i // tps, 0, i % tps, 0))
    vmem = (2 * tm * d * 4 + d * n * 2 + 2 * tm * n * 2 + 8 * tm * COL_CHUNK * 4 + 2 * len(tables) * tm * LANES * 4
            + n_perm * tm * d * (4 + 2) + (8 << 20))
    return pl.pallas_call(
        functools.partial(_att_proj_kernel, dilations=dilations, n_rope=n_rope),
        grid=(tokens // tm,),
        in_specs=[pl.BlockSpec((tm, d), row), _resident((1, d)), _resident(w.shape, layer)] + [tab] * len(tables),
        out_specs=[plane(dil) for dil in dilations],
        out_shape=[jax.ShapeDtypeStruct((batch, dil, seq // dil, gw), BF16) for dil in dilations],
        scratch_shapes=[pltpu.VMEM((n_perm, d // LANES, tm, LANES), F32), pltpu.VMEM((n_perm, tm, d), BF16)],
        compiler_params=_params(1, vmem),
        name="att_proj",
    )(x, gain, w, *tables)


def _dil_attn_kernel(q_ref, k_ref, v_ref, o_ref, m_ref, d_ref, bias_first_ref, bias_band_ref):
    n_planes, length = q_ref.shape[0], q_ref.shape[1]
    blk = ATT_BLOCK
    n_blocks = length // blk
    n_pairs = q_ref.shape[2] // LANES
    lane = lax.broadcasted_iota(jnp.int32, (blk, LANES), 1)
    low = lane < ATT_HEAD_DIM
    is_b, _ = _qk_lane_layout()
    edges = [0] + [i for i in range(1, LANES) if is_b[i] != is_b[i - 1]] + [LANES]
    q_first = None
    for lo, hi in zip(edges[:-1], edges[1:]):
        if not is_b[lo]:
            run = jnp.logical_and(lane >= lo, lane < hi)
            q_first = run if q_first is None else jnp.logical_or(q_first, run)
    zero = jnp.zeros((blk, LANES), BF16)

    def bias(n_keys):
        qi = lax.broadcasted_iota(jnp.int32, (2 * blk, n_keys), 0) % blk
        kj = lax.broadcasted_iota(jnp.int32, (2 * blk, n_keys), 1)
        dist = qi + (n_keys - blk) - kj
        return jnp.where(jnp.logical_and(dist >= 0, dist <= blk), 0.0, NEG_BIG).astype(F32)

    bias_first_ref[...] = bias(blk)
    bias_band_ref[...] = bias(2 * blk)

    def scores(plane, q_rows, k_rows, bias_ref, hp):
        cols = slice(hp * LANES, (hp + 1) * LANES)
        q2 = q_ref[plane, q_rows, cols]
        qq = jnp.concatenate([jnp.where(q_first, q2, zero), jnp.where(q_first, zero, q2)], axis=0)
        return _dot_nt(qq, k_ref[plane, k_rows, cols]) + bias_ref[...]

    def softmax(s):
        m = jnp.max(s, axis=-1, keepdims=True)
        return m, jnp.exp2(s - m).astype(BF16)

    def finish(plane, q_rows, k_rows, bias_ref, hp, mp):
        m, p = mp
        cols = slice(hp * LANES, (hp + 1) * LANES)
        ones = jnp.ones((bias_ref.shape[1], LANES), BF16)
        pv = _dot(p, jnp.concatenate([v_ref[plane, k_rows, cols], ones], axis=1))
        o_ref[plane, q_rows, cols] = jnp.where(low, pv[:blk, :LANES], pv[blk:, :LANES]).astype(BF16)
        for half in range(2):
            head = 2 * hp + half
            rows = slice(half * blk, (half + 1) * blk)
            m_ref[plane, q_rows, head:head + 1] = m[rows]
            d_ref[plane, q_rows, head:head + 1] = pv[rows, LANES + head:LANES + head + 1]

    def run(items):
        work = [(*item, hp) for item in items for hp in range(n_pairs)]
        for plane, q_rows, _, _ in items:
            m_ref[plane, q_rows, :] = jnp.zeros((blk, LANES), F32)
            d_ref[plane, q_rows, :] = jnp.ones((blk, LANES), F32)
        n = len(work)
        s_vals, mp_vals = {}, {}
        for t in range(-ATT_SKEW, n):
            if t + ATT_SKEW < n:
                s_vals[t + ATT_SKEW] = scores(*work[t + ATT_SKEW])
            mid = t + ATT_SKEW // 2
            if 0 <= mid < n and ATT_SKEW > 1:
                mp_vals[mid] = softmax(s_vals.pop(mid))
            if t >= 0:
                if ATT_SKEW <= 1:
                    mp_vals[t] = softmax(s_vals.pop(t))
                finish(*work[t], mp_vals.pop(t))

    first = [(plane, pl.ds(0, blk), pl.ds(0, blk), bias_first_ref) for plane in range(n_planes)]
    n_band = n_blocks - 1
    unroll = max(u for u in range(1, ATT_UNROLL + 1) if n_band % u == 0) if n_band else 0

    def band(plane, i):
        items = []
        for j in range(unroll):
            start = (i * unroll + j + 1) * blk
            start = start if isinstance(start, int) else pl.multiple_of(start, blk)
            items.append((plane, pl.ds(start, blk), pl.ds(start - blk, 2 * blk), bias_band_ref))
        return items

    if n_band == unroll:
        run(first + [item for plane in range(n_planes) for item in band(plane, 0)])
    else:
        run(first)
        for plane in range(n_planes):
            def later_blocks(i, carry, plane=plane):
                run(band(plane, i))
                return carry

            lax.fori_loop(0, n_band // unroll, later_blocks, 0)


def _dil_attn(q, k, v):
    batch, dil, length, gw = q.shape
    planes = min(dil, max(1, ATT_MIN_BLOCKS * ATT_BLOCK // length))
    spec = lambda width: pl.BlockSpec((None, planes, length, width), lambda b, r: (b, r, 0, 0))
    blk = ATT_BLOCK
    vmem = (2 * 4 * planes * length * gw * 2 + 4 * planes * length * LANES * 4 + 3 * 2 * blk * 2 * blk * 4
            + 48 * blk * 2 * blk * 4 + (8 << 20))
    stat = jax.ShapeDtypeStruct((batch, dil, length, LANES), F32)
    return pl.pallas_call(
        _dil_attn_kernel,
        grid=(batch, dil // planes),
        in_specs=[spec(gw), spec(gw), spec(gw)],
        out_specs=[spec(gw), spec(LANES), spec(LANES)],
        out_shape=[jax.ShapeDtypeStruct((batch, dil, length, gw), BF16), stat, stat],
        scratch_shapes=[pltpu.VMEM((2 * blk, blk), F32), pltpu.VMEM((2 * blk, 2 * blk), F32)],
        compiler_params=_params(2, vmem),
        name=f"dil_attn_d{dil}",
    )(q, k, v)


def kernel(x, ret_w_in, ret_w_out, ret_gn_gain, kv_norm, att_w_kv, att_w_q, att_w_o, norm_mix_pre, norm_mix_post, norm_ffn_pre, norm_ffn_post, ffn_w_up, ffn_conv_w, ffn_conv_b, ffn_w_down):
    batch, seq, d = x.shape
    depth = norm_mix_pre.shape[0]
    n_ret = ret_w_in.shape[0]
    assert seq % TOKEN_TILE == 0 and seq % WIDE_TILE == 0 and seq % RET_CHUNK == 0
    assert all(seq % (dil * ATT_BLOCK) == 0 and TOKEN_TILE % (16 * dil) == 0 for dil in DILATIONS)
    assert all(win // dil == ATT_BLOCK for win, dil in DIL_GROUPS)

    rcos, rsin, att_tables = _rope_tables(seq)
    q_scale = ATT_HEAD_DIM ** -0.5 * math.log2(math.e)
    n_qk = N_GROUPS * ATT_GROUP_WIDTH
    log_gamma = jnp.log(1.0 - 2.0 ** (-5.0 - jnp.arange(RET_HEADS, dtype=F32)))
    gain = lambda g: g[None, :]
    bf = lambda w: w.astype(BF16)
    w_in, w_out, w_o = bf(ret_w_in), bf(ret_w_out), bf(att_w_o)
    w_q = bf(_qk_slab_columns(att_w_q) * q_scale)
    w_kv = bf(jnp.concatenate([_qk_slab_columns(att_w_kv[:, :n_qk]), att_w_kv[:, n_qk:]], axis=1))
    w_up, w_down = bf(ffn_w_up), bf(ffn_w_down)

    xt = x.reshape(batch * seq, d)
    keys = values = None
    for layer in range(depth):
        if layer < n_ret:
            y = _ret_mixer(xt, gain(norm_mix_pre[layer]), w_in, layer, rcos, rsin,
                           gain(ret_gn_gain[layer]), log_gamma, batch, seq)
            xt = _outproj(xt, y, w_out, layer, gain(norm_mix_post[layer]))
        else:
            bi = layer - n_ret
            queries = _att_proj(xt, gain(norm_mix_pre[layer]), w_q, bi, att_tables, N_GROUPS, batch, seq)
            outs, maxes, dens = zip(*[_dil_attn(queries[gi], keys[gi], values[gi]) for gi in range(N_GROUPS)])
            xt = _att_merge(xt, outs, maxes, dens, w_o, bi, gain(norm_mix_post[layer]), seq)
        xt = _ffn(xt, gain(norm_ffn_pre[layer]), w_up, ffn_conv_w[layer], gain(ffn_conv_b[layer]),
                  w_down, layer, gain(norm_ffn_post[layer]), seq)
        if layer == n_ret - 1:
            kv = _att_proj(xt, gain(kv_norm), w_kv, None, att_tables, N_GROUPS, batch, seq)
            keys, values = kv[:N_GROUPS], kv[N_GROUPS:]
    return xt.reshape(batch, seq, d)
```
